```python
import jax, jax.numpy as jnp
from jax import lax
import numpy as np

D_MODEL = 2048
BATCH = 4
SEQ = 2048
DEPTH = 1

PLE_DIM = 256
MIX_W = D_MODEL
HGRN_W = MIX_W // 2
CONV_W = MIX_W - HGRN_W
HGRN_HEAD_DIM = 128
HGRN_HEADS = HGRN_W // HGRN_HEAD_DIM
HGRN_CHUNK = 64
CONV_K = 3
CONV_GROUPS = 8
D_FF = 5632
EPS = 1e-6
MIX_IN = 4 * HGRN_W + 3 * CONV_W
SPLITS = [HGRN_W, 2 * HGRN_W, 3 * HGRN_W, 4 * HGRN_W,
          4 * HGRN_W + CONV_W, 4 * HGRN_W + 2 * CONV_W]

kernel_name = "hymba_hgrn2_shortconv_macaron"


def rms_norm(x, g):
    xf = x.astype(jnp.float32)
    r = lax.rsqrt(jnp.mean(xf * xf, axis=-1, keepdims=True) + EPS)
    return (xf * r).astype(x.dtype) * g


def swiglu(x, w_gate, w_up, w_down):
    return (jax.nn.silu(x @ w_gate) * (x @ w_up)) @ w_down


def hgrn2_chunked(q, f_logit, v, lb):
    bsz, s, h, dh = q.shape
    nc = s // HGRN_CHUNK
    qf = jax.nn.silu(q.astype(jnp.float32))
    f = lb + (1.0 - lb) * jax.nn.sigmoid(f_logit.astype(jnp.float32))
    k = 1.0 - f
    log_f = jnp.log(f)
    vf = v.astype(jnp.float32)

    def to_chunks(a):
        return a.reshape(bsz, nc, HGRN_CHUNK, h, dh).transpose(1, 0, 3, 2, 4)

    qc, kc, vc, lc = to_chunks(qf), to_chunks(k), to_chunks(vf), to_chunks(log_f)
    bc = jnp.cumsum(lc, axis=-2)
    causal = jnp.tril(jnp.ones((HGRN_CHUNK, HGRN_CHUNK), dtype=bool))

    def step(state, xs):
        qx, kx, vx, bx = xs
        rel = bx[..., :, None, :] - bx[..., None, :, :]
        decay = jnp.exp(jnp.where(causal[:, :, None], rel, -jnp.inf))
        scores = jnp.einsum('bhtk,bhsk,bhtsk->bhts', qx, kx, decay)
        o = (jnp.einsum('bhts,bhsv->bhtv', scores, vx)
             + jnp.einsum('bhtk,bhkv->bhtv', qx * jnp.exp(bx), state))
        b_last = bx[..., -1:, :]
        state = (jnp.exp(b_last)[..., 0, :, None] * state
                 + jnp.einsum('bhsk,bhsv->bhkv', kx * jnp.exp(b_last - bx), vx))
        return state, o

    s0 = jnp.zeros((bsz, h, dh, dh), jnp.float32)
    _, o = lax.scan(step, s0, (qc, kc, vc, bc))
    return o.transpose(1, 0, 3, 2, 4).reshape(bsz, s, h, dh).astype(q.dtype)


def short_conv(b_gate, c_gate, v, w_conv):
    u = c_gate * v
    up = jnp.pad(u, ((0, 0), (CONV_K - 1, 0), (0, 0)))
    s = u.shape[1]
    y = w_conv[0] * up[:, 0:s] + w_conv[1] * up[:, 1:s + 1] + w_conv[2] * up[:, 2:s + 2]
    return b_gate * y


def setup_inputs(seed: int = 0) -> dict:
    key = jax.random.key(seed)
    ks = jax.random.split(key, 24)

    def w(k, shape, fan_in):
        return jax.random.normal(k, shape, jnp.float32) * (fan_in ** -0.5)

    def gain(k, shape):
        return 1.0 + 0.02 * jax.random.normal(k, shape, jnp.float32)

    return {
        "x": jax.random.normal(ks[0], (BATCH, SEQ, D_MODEL), jnp.float32),
        "p": jax.random.normal(ks[1], (DEPTH, BATCH, SEQ, PLE_DIM), jnp.float32),
        "norm_ffn1": gain(ks[2], (DEPTH, D_MODEL)),
        "ffn1_gate": w(ks[3], (DEPTH, D_MODEL, D_FF), D_MODEL),
        "ffn1_up": w(ks[4], (DEPTH, D_MODEL, D_FF), D_MODEL),
        "ffn1_down": w(ks[5], (DEPTH, D_FF, D_MODEL), D_FF),
        "norm_mix": gain(ks[6], (DEPTH, D_MODEL)),
        "w_in": w(ks[7], (DEPTH, D_MODEL, MIX_IN), D_MODEL),
        "conv_w": w(ks[8], (DEPTH, CONV_K, CONV_W), CONV_K),
        "hgrn_lb_logits": jax.random.normal(ks[9], (DEPTH + 1, HGRN_W), jnp.float32),
        "hgrn_norm": gain(ks[10], (DEPTH, HGRN_HEAD_DIM)),
        "conv_norm": gain(ks[11], (DEPTH, CONV_W)),
        "w_out": w(ks[12], (DEPTH, MIX_W, D_MODEL), MIX_W),
        "norm_ffn2": gain(ks[13], (DEPTH, D_MODEL)),
        "ffn2_gate": w(ks[14], (DEPTH, D_MODEL, D_FF), D_MODEL),
        "ffn2_up": w(ks[15], (DEPTH, D_MODEL, D_FF), D_MODEL),
        "ffn2_down": w(ks[16], (DEPTH, D_FF, D_MODEL), D_FF),
        "norm_ple": gain(ks[17], (DEPTH, D_MODEL)),
        "w_ple": w(ks[18], (DEPTH, PLE_DIM, D_MODEL), PLE_DIM),
        "w_ple_gate": w(ks[19], (DEPTH, D_MODEL, D_MODEL), D_MODEL),
        "norm_final": gain(ks[20], (D_MODEL,)),
    }


def reference(x, p, norm_ffn1, ffn1_gate, ffn1_up, ffn1_down, norm_mix, w_in, conv_w,
              hgrn_lb_logits, hgrn_norm, conv_norm, w_out, norm_ffn2, ffn2_gate, ffn2_up,
              ffn2_down, norm_ple, w_ple, w_ple_gate, norm_final):
    bsz, s, _ = x.shape
    lb_all = jnp.cumsum(jax.nn.softmax(hgrn_lb_logits.astype(jnp.float32), axis=0), axis=0)
    h = x
    for i in range(DEPTH):
        h = h + 0.5 * swiglu(rms_norm(h, norm_ffn1[i]), ffn1_gate[i], ffn1_up[i], ffn1_down[i])

        u = rms_norm(h, norm_mix[i])
        proj = u @ w_in[i]
        q, f_logit, v_h, g_h, b_c, c_c, v_c = jnp.split(proj, SPLITS, axis=-1)

        hd = (bsz, s, HGRN_HEADS, HGRN_HEAD_DIM)
        lb = lb_all[i].reshape(HGRN_HEADS, HGRN_HEAD_DIM)
        o_h = hgrn2_chunked(q.reshape(hd), f_logit.reshape(hd), v_h.reshape(hd), lb)
        o_h = rms_norm(o_h, hgrn_norm[i]) * jax.nn.silu(g_h.reshape(hd))
        o_h = o_h.reshape(bsz, s, HGRN_W)

        y_c = short_conv(b_c, c_c, v_c, conv_w[i])
        y_c = rms_norm(y_c.reshape(bsz, s, CONV_GROUPS, CONV_W // CONV_GROUPS),
                       conv_norm[i].reshape(CONV_GROUPS, CONV_W // CONV_GROUPS)).reshape(bsz, s, CONV_W)

        h = h + jnp.concatenate([o_h, y_c], axis=-1) @ w_out[i]

        h = h + 0.5 * swiglu(rms_norm(h, norm_ffn2[i]), ffn2_gate[i], ffn2_up[i], ffn2_down[i])

        gate = jax.nn.sigmoid(rms_norm(h, norm_ple[i]) @ w_ple_gate[i])
        h = h + gate * (p[i] @ w_ple[i])
    return rms_norm(h, norm_final)
```

```python
import functools

import jax
import jax.numpy as jnp
from jax import lax
from jax.experimental import pallas as pl
from jax.experimental.pallas import tpu as pltpu

EPS = 1e-6
HEAD_DIM = 128
SUBLANES = 8
VMEM_LIMIT_BYTES = 56 * 1024 * 1024

F32 = jnp.float32
BF16 = jnp.bfloat16


def _rms(x, gain):
    r = lax.rsqrt(jnp.mean(x * x, axis=-1, keepdims=True) + EPS)
    return (x * r) * gain


def _dot(a, b):
    return jnp.dot(a, b, preferred_element_type=F32)


def _dot_nt(a, b):
    return lax.dot_general(a, b, (((1,), (1,)), ((), ())), preferred_element_type=F32)


def _dot_tn(a, b):
    return lax.dot_general(a, b, (((0,), (0,)), ((), ())), preferred_element_type=F32)


def _ffn_kernel(x_ref, g_ref, wg_ref, wu_ref, wd_ref, gnext_ref, h_ref, n_ref, xn_ref):
    f = pl.program_id(1)

    @pl.when(f == 0)
    def _():
        xn_ref[...] = _rms(x_ref[...], g_ref[...]).astype(BF16)
        h_ref[...] = jnp.zeros_like(h_ref)

    xn = xn_ref[...]
    gate = _dot(xn, wg_ref[...])
    up = _dot(xn, wu_ref[...])
    act = (gate * jax.nn.sigmoid(gate) * up).astype(BF16)
    h_ref[...] += _dot(act, wd_ref[...])

    @pl.when(f == pl.num_programs(1) - 1)
    def _():
        h = x_ref[...] + 0.5 * h_ref[...]
        h_ref[...] = h
        n_ref[...] = _rms(h, gnext_ref[...]).astype(BF16)


def _ffn(x, gain, wg, wu, wd, gain_next, *, tm, tf):
    t, d = x.shape
    dff = wg.shape[1]
    return pl.pallas_call(
        _ffn_kernel,
        grid=(t // tm, dff // tf),
        in_specs=[
            pl.BlockSpec((tm, d), lambda i, f: (i, 0)),
            pl.BlockSpec((1, d), lambda i, f: (0, 0)),
            pl.BlockSpec((d, tf), lambda i, f: (0, f)),
            pl.BlockSpec((d, tf), lambda i, f: (0, f)),
            pl.BlockSpec((tf, d), lambda i, f: (f, 0)),
            pl.BlockSpec((1, d), lambda i, f: (0, 0)),
        ],
        out_specs=[
            pl.BlockSpec((tm, d), lambda i, f: (i, 0)),
            pl.BlockSpec((tm, d), lambda i, f: (i, 0)),
        ],
        out_shape=[
            jax.ShapeDtypeStruct((t, d), F32),
            jax.ShapeDtypeStruct((t, d), BF16),
        ],
        scratch_shapes=[pltpu.VMEM((tm, d), BF16)],
        compiler_params=pltpu.CompilerParams(
            dimension_semantics=("arbitrary", "arbitrary"),
            vmem_limit_bytes=VMEM_LIMIT_BYTES),
        name="ffn",
    )(x, gain, wg, wu, wd, gain_next)


def _cumsum_rows(x):
    n = x.shape[0]
    row = lax.broadcasted_iota(jnp.int32, x.shape, 0)
    sh = 1
    while sh < n:
        x = x + jnp.where(row >= sh, pltpu.roll(x, sh, axis=0), 0.0)
        sh *= 2
    return x


def _bcast_row(x, blk, r):
    c, l = x.shape
    x3 = x.reshape(c // blk, blk, l)
    return jnp.broadcast_to(x3[:, r:r + 1, :], (c // blk, blk, l)).reshape(c, l)


def _hgrn_chunk(q, fl, v, lb, state_t):
    c = q.shape[0]
    f = lb + (1.0 - lb) * jax.nn.sigmoid(fl)
    k = 1.0 - f
    qf = q * jax.nn.sigmoid(q)
    b = _cumsum_rows(jnp.log(f))

    row = lax.broadcasted_iota(jnp.int32, (c, HEAD_DIM), 0)
    tt = lax.broadcasted_iota(jnp.int32, (c, c), 0)
    ss = lax.broadcasted_iota(jnp.int32, (c, c), 1)

    scores = jnp.zeros((c, c), F32)
    m = 2 * SUBLANES
    while m <= c:
        half = m // 2
        bref = _bcast_row(b, m, half - 1)
        later = (row % m) >= half
        e = jnp.exp(jnp.where(later, b - bref, bref - b))
        ql = jnp.where(later, qf * e, 0.0).astype(BF16)
        kl = jnp.where(later, 0.0, k * e).astype(BF16)
        sl = _dot_nt(ql, kl)
        scores = scores + jnp.where((tt // m) == (ss // m), sl, 0.0)
        m *= 2
    o = _dot(scores.astype(BF16), v.astype(BF16))

    sub = row % SUBLANES
    for j in range(SUBLANES):
        valid = sub >= j
        e = jnp.exp(jnp.where(valid, b - _bcast_row(b, SUBLANES, j), 0.0))
        term = jnp.where(valid, qf * _bcast_row(k, SUBLANES, j) * e, 0.0)
        o = o + jnp.sum(term, axis=-1, keepdims=True) * _bcast_row(v, SUBLANES, j)

    o = o + _dot_nt((qf * jnp.exp(b)).astype(BF16), state_t.astype(BF16))
    b_last = b[c - 1:c, :]
    kd = (k * jnp.exp(b_last - b)).astype(BF16)
    state_t = state_t * jnp.exp(b_last) + _dot_tn(v.astype(BF16), kd)
    return o, state_t


def _hgrn_kernel(u_ref, w_ref, lbl_ref, gn_ref, o_ref, state_ref, *, chunk):
    @pl.when(pl.program_id(2) == 0)
    def _():
        state_ref[...] = jnp.zeros_like(state_ref)

    proj = _dot(u_ref[0], w_ref[...])
    logits = [lbl_ref[l, 0] for l in range(lbl_ref.shape[0])]
    mx = functools.reduce(jnp.maximum, logits)
    ex = [jnp.exp(l - mx) for l in logits]
    lb = ex[0] / functools.reduce(jnp.add, ex)

    ts = proj.shape[0]
    state_t = state_ref[...]
    for c0 in range(0, ts, chunk):
        p = proj[c0:c0 + chunk]
        q = p[:, 0 * HEAD_DIM:1 * HEAD_DIM]
        fl = p[:, 1 * HEAD_DIM:2 * HEAD_DIM]
        v = p[:, 2 * HEAD_DIM:3 * HEAD_DIM]
        g = p[:, 3 * HEAD_DIM:4 * HEAD_DIM]
        o, state_t = _hgrn_chunk(q, fl, v, lb, state_t)
        o = _rms(o, gn_ref[...]) * (g * jax.nn.sigmoid(g))
        o_ref[0, c0:c0 + chunk, :] = o.astype(o_ref.dtype)
    state_ref[...] = state_t


def _hgrn(u, w_h, lb_logits, gn, *, ts, chunk):
    bsz, s, d = u.shape
    heads = w_h.shape[1] // (4 * HEAD_DIM)
    nl = lb_logits.shape[0]
    return pl.pallas_call(
        functools.partial(_hgrn_kernel, chunk=chunk),
        grid=(bsz, heads, s // ts),
        in_specs=[
            pl.BlockSpec((1, ts, d), lambda b, h, i: (b, i, 0)),
            pl.BlockSpec((d, 4 * HEAD_DIM), lambda b, h, i: (0, h)),
            pl.BlockSpec((nl, 1, 1, HEAD_DIM), lambda b, h, i: (0, h, 0, 0)),
            pl.BlockSpec((1, HEAD_DIM), lambda b, h, i: (0, 0)),
        ],
        out_specs=pl.BlockSpec((1, ts, HEAD_DIM), lambda b, h, i: (b, i, h)),
        out_shape=jax.ShapeDtypeStruct((bsz, s, heads * HEAD_DIM), BF16),
        scratch_shapes=[pltpu.VMEM((HEAD_DIM, HEAD_DIM), F32)],
        compiler_params=pltpu.CompilerParams(
            dimension_semantics=("arbitrary", "arbitrary", "arbitrary"),
            vmem_limit_bytes=VMEM_LIMIT_BYTES),
        name="hgrn",
    )(u, w_h, lb_logits, gn)


def _conv_kernel(u_ref, w_ref, cw_ref, gn_ref, o_ref, halo_ref):
    @pl.when(pl.program_id(1) == 0)
    def _():
        halo_ref[...] = jnp.zeros_like(halo_ref)

    proj = _dot(u_ref[0], w_ref[...])
    cw = proj.shape[1] // 3
    ts = proj.shape[0]
    bg = proj[:, 0:cw]
    uc = proj[:, cw:2 * cw] * proj[:, 2 * cw:3 * cw]
    row = lax.broadcasted_iota(jnp.int32, (ts, cw), 0)
    prev1 = halo_ref[1:2, :]
    prev2 = halo_ref[0:1, :]
    u1 = jnp.where(row >= 1, pltpu.roll(uc, 1, axis=0), prev1)
    u2 = jnp.where(row >= 2, pltpu.roll(uc, 2, axis=0), jnp.where(row == 1, prev1, prev2))
    halo_ref[...] = uc[ts - 2:ts, :]
    y = bg * (cw_ref[0:1, :] * u2 + cw_ref[1:2, :] * u1 + cw_ref[2:3, :] * uc)
    for g0 in range(0, cw, HEAD_DIM):
        yg = y[:, g0:g0 + HEAD_DIM]
        o_ref[0, :, g0:g0 + HEAD_DIM] = _rms(yg, gn_ref[:, g0:g0 + HEAD_DIM]).astype(o_ref.dtype)


def _conv(u, w_c, conv_w, gn, *, ts):
    bsz, s, d = u.shape
    cw = w_c.shape[1] // 3
    return pl.pallas_call(
        _conv_kernel,
        grid=(bsz, s // ts),
        in_specs=[
            pl.BlockSpec((1, ts, d), lambda b, i: (b, i, 0)),
            pl.BlockSpec((d, 3 * cw), lambda b, i: (0, 0)),
            pl.BlockSpec((3, cw), lambda b, i: (0, 0)),
            pl.BlockSpec((1, cw), lambda b, i: (0, 0)),
        ],
        out_specs=pl.BlockSpec((1, ts, cw), lambda b, i: (b, i, 0)),
        out_shape=jax.ShapeDtypeStruct((bsz, s, cw), BF16),
        scratch_shapes=[pltpu.VMEM((2, cw), F32)],
        compiler_params=pltpu.CompilerParams(
            dimension_semantics=("arbitrary", "arbitrary"),
            vmem_limit_bytes=VMEM_LIMIT_BYTES),
        name="conv",
    )(u, w_c, conv_w, gn)


def _outproj_kernel(h_ref, a_ref, b_ref, wa_ref, wb_ref, o_ref):
    o_ref[...] = h_ref[...] + _dot(a_ref[...], wa_ref[...]) + _dot(b_ref[...], wb_ref[...])


def _outproj(h, a, b, wa, wb, *, tm):
    t, d = h.shape
    ka, kb = a.shape[1], b.shape[1]
    return pl.pallas_call(
        _outproj_kernel,
        grid=(t // tm,),
        in_specs=[
            pl.BlockSpec((tm, d), lambda i: (i, 0)),
            pl.BlockSpec((tm, ka), lambda i: (i, 0)),
            pl.BlockSpec((tm, kb), lambda i: (i, 0)),
            pl.BlockSpec((ka, d), lambda i: (0, 0)),
            pl.BlockSpec((kb, d), lambda i: (0, 0)),
        ],
        out_specs=pl.BlockSpec((tm, d), lambda i: (i, 0)),
        out_shape=jax.ShapeDtypeStruct((t, d), F32),
        compiler_params=pltpu.CompilerParams(
            dimension_semantics=("arbitrary",),
            vmem_limit_bytes=VMEM_LIMIT_BYTES),
        name="outproj",
    )(h, a, b, wa, wb)


def _ple_kernel(h_ref, n_ref, p_ref, wg_ref, we_ref, gf_ref, o_ref):
    gate = jax.nn.sigmoid(_dot(n_ref[...], wg_ref[...]))
    pe = _dot(p_ref[...].astype(BF16), we_ref[...])
    o_ref[...] = _rms(h_ref[...] + gate * pe, gf_ref[...])


def _ple(h, n, p, wg, we, gf, *, tm):
    t, d = h.shape
    pd = p.shape[1]
    return pl.pallas_call(
        _ple_kernel,
        grid=(t // tm,),
        in_specs=[
            pl.BlockSpec((tm, d), lambda i: (i, 0)),
            pl.BlockSpec((tm, d), lambda i: (i, 0)),
            pl.BlockSpec((tm, pd), lambda i: (i, 0)),
            pl.BlockSpec((d, d), lambda i: (0, 0)),
            pl.BlockSpec((pd, d), lambda i: (0, 0)),
            pl.BlockSpec((1, d), lambda i: (0, 0)),
        ],
        out_specs=pl.BlockSpec((tm, d), lambda i: (i, 0)),
        out_shape=jax.ShapeDtypeStruct((t, d), F32),
        compiler_params=pltpu.CompilerParams(
            dimension_semantics=("arbitrary",),
            vmem_limit_bytes=VMEM_LIMIT_BYTES),
        name="ple",
    )(h, n, p, wg, we, gf)


def kernel(x, p, norm_ffn1, ffn1_gate, ffn1_up, ffn1_down, norm_mix, w_in, conv_w, hgrn_lb_logits,
           hgrn_norm, conv_norm, w_out, norm_ffn2, ffn2_gate, ffn2_up, ffn2_down, norm_ple, w_ple,
           w_ple_gate, norm_final):
    bsz, s, d = x.shape
    depth = w_in.shape[0]
    assert depth == 1, "single-layer block"
    t = bsz * s
    hw = hgrn_lb_logits.shape[1]
    heads = hw // HEAD_DIM
    cw = conv_w.shape[2]
    assert w_in.shape[2] == 4 * hw + 3 * cw

    row = lambda a: a.reshape(1, -1)
    i = 0
    w_h = w_in[i][:, :4 * hw].reshape(d, 4, heads, HEAD_DIM).transpose(0, 2, 1, 3).reshape(d, 4 * hw).astype(BF16)
    w_c = w_in[i][:, 4 * hw:].astype(BF16)
    lb_logits = hgrn_lb_logits.reshape(-1, heads, 1, HEAD_DIM)

    h1, u = _ffn(x.reshape(t, d), row(norm_ffn1[i]), ffn1_gate[i].astype(BF16), ffn1_up[i].astype(BF16),
                 ffn1_down[i].astype(BF16), row(norm_mix[i]), tm=512, tf=512)
    u3 = u.reshape(bsz, s, d)
    o_h = _hgrn(u3, w_h, lb_logits, row(hgrn_norm[i]), ts=512, chunk=128)
    y_c = _conv(u3, w_c, conv_w[i], row(conv_norm[i]), ts=512)
    wo = w_out[i].astype(BF16)
    h2 = _outproj(h1, o_h.reshape(t, hw), y_c.reshape(t, cw), wo[:hw], wo[hw:], tm=512)
    h3, n3 = _ffn(h2, row(norm_ffn2[i]), ffn2_gate[i].astype(BF16), ffn2_up[i].astype(BF16),
                  ffn2_down[i].astype(BF16), row(norm_ple[i]), tm=512, tf=512)
    y = _ple(h3, n3, p[i].reshape(t, -1), w_ple_gate[i].astype(BF16), w_ple[i].astype(BF16),
             row(norm_final), tm=512)
    return y.reshape(bsz, s, d)
```

```python
import functools

import jax
import jax.numpy as jnp
from jax import lax
from jax.experimental import pallas as pl
from jax.experimental.pallas import tpu as pltpu

EPS = 1e-6
HEAD_DIM = 128
SUBLANES = 8
VMEM_LIMIT_BYTES = 56 * 1024 * 1024
FFN_VMEM_LIMIT_BYTES = 60 * 1024 * 1024

F32 = jnp.float32
BF16 = jnp.bfloat16


def _rms(x, gain):
    r = lax.rsqrt(jnp.mean(x * x, axis=-1, keepdims=True) + EPS)
    return (x * r) * gain


def _dot(a, b):
    return jnp.dot(a, b, preferred_element_type=F32)


def _dot_nt(a, b):
    return lax.dot_general(a, b, (((1,), (1,)), ((), ())), preferred_element_type=F32)


def _dot_tn(a, b):
    return lax.dot_general(a, b, (((0,), (0,)), ((), ())), preferred_element_type=F32)


def _resident(shape, index_map):
    return pl.BlockSpec(shape, index_map, pipeline_mode=pl.Buffered(1))


def _ffn_kernel(x_ref, g_ref, wg_ref, wu_ref, wd_ref, gnext_ref, h_ref, n_ref, xn_ref):
    f = pl.program_id(1)

    @pl.when(f == 0)
    def _():
        x = x_ref[...]
        xn_ref[...] = _rms(x, g_ref[...]).astype(BF16)
        h_ref[...] = x

    xn = xn_ref[...]
    gate = _dot(xn, wg_ref[...].astype(BF16))
    up = _dot(xn, wu_ref[...].astype(BF16))
    act = ((0.5 * gate) * jax.nn.sigmoid(gate) * up).astype(BF16)
    h_ref[...] += _dot(act, wd_ref[...].astype(BF16))

    @pl.when(f == pl.num_programs(1) - 1)
    def _():
        n_ref[...] = _rms(h_ref[...], gnext_ref[...]).astype(BF16)


def _ffn(x, gain, wg, wu, wd, gain_next, *, tm, tf):
    t, d = x.shape
    dff = wg.shape[1]
    return pl.pallas_call(
        _ffn_kernel,
        grid=(t // tm, dff // tf),
        in_specs=[
            _resident((tm, d), lambda i, f: (i, 0)),
            pl.BlockSpec((1, d), lambda i, f: (0, 0)),
            pl.BlockSpec((d, tf), lambda i, f: (0, f)),
            pl.BlockSpec((d, tf), lambda i, f: (0, f)),
            pl.BlockSpec((tf, d), lambda i, f: (f, 0)),
            pl.BlockSpec((1, d), lambda i, f: (0, 0)),
        ],
        out_specs=[
            pl.BlockSpec((tm, d), lambda i, f: (i, 0)),
            pl.BlockSpec((tm, d), lambda i, f: (i, 0)),
        ],
        out_shape=[
            jax.ShapeDtypeStruct((t, d), F32),
            jax.ShapeDtypeStruct((t, d), BF16),
        ],
        scratch_shapes=[pltpu.VMEM((tm, d), BF16)],
        compiler_params=pltpu.CompilerParams(
            dimension_semantics=("arbitrary", "arbitrary"),
            vmem_limit_bytes=FFN_VMEM_LIMIT_BYTES),
        name="ffn",
    )(x, gain, wg, wu, wd, gain_next)


def _cumsum_rows(x):
    n = x.shape[0]
    row = lax.broadcasted_iota(jnp.int32, x.shape, 0)
    sh = 1
    while sh < n:
        x = x + jnp.where(row >= sh, pltpu.roll(x, sh, axis=0), 0.0)
        sh *= 2
    return x


def _bcast_row(x, blk, r):
    c, l = x.shape
    x3 = x.reshape(c // blk, blk, l)
    return jnp.broadcast_to(x3[:, r:r + 1, :], (c // blk, blk, l)).reshape(c, l)


def _hgrn_chunk(q, fl, v, lb, state_t):
    c = q.shape[0]
    f = lb + (1.0 - lb) * jax.nn.sigmoid(fl)
    k = 1.0 - f
    qf = q * jax.nn.sigmoid(q)
    b = _cumsum_rows(jnp.log(f))

    row = lax.broadcasted_iota(jnp.int32, (c, HEAD_DIM), 0)
    tt = lax.broadcasted_iota(jnp.int32, (c, c), 0)
    ss = lax.broadcasted_iota(jnp.int32, (c, c), 1)

    scores = jnp.zeros((c, c), F32)
    m = 2 * SUBLANES
    while m <= c:
        half = m // 2
        bref = _bcast_row(b, m, half - 1)
        later = (row % m) >= half
        e = jnp.exp(jnp.where(later, b - bref, bref - b))
        ql = jnp.where(later, qf * e, 0.0).astype(BF16)
        kl = jnp.where(later, 0.0, k * e).astype(BF16)
        sl = _dot_nt(ql, kl)
        scores = scores + jnp.where((tt // m) == (ss // m), sl, 0.0)
        m *= 2
    o = _dot(scores.astype(BF16), v.astype(BF16))

    sub = row % SUBLANES
    for j in range(SUBLANES):
        valid = sub >= j
        e = jnp.exp(jnp.where(valid, b - _bcast_row(b, SUBLANES, j), 0.0))
        term = jnp.where(valid, qf * _bcast_row(k, SUBLANES, j) * e, 0.0)
        o = o + jnp.sum(term, axis=-1, keepdims=True) * _bcast_row(v, SUBLANES, j)

    o = o + _dot_nt((qf * jnp.exp(b)).astype(BF16), state_t.astype(BF16))
    b_last = b[c - 1:c, :]
    kd = (k * jnp.exp(b_last - b)).astype(BF16)
    state_t = state_t * jnp.exp(b_last) + _dot_tn(v.astype(BF16), kd)
    return o, state_t


def _hgrn_kernel(u_ref, wq_ref, wf_ref, wi_ref, wg_ref, lbl_ref, gn_ref, o_ref, w_ref, state_ref, *, chunk):
    @pl.when(pl.program_id(2) == 0)
    def _():
        state_ref[...] = jnp.zeros_like(state_ref)
        for j, src in enumerate((wq_ref, wf_ref, wi_ref, wg_ref)):
            w_ref[:, j * HEAD_DIM:(j + 1) * HEAD_DIM] = src[...].astype(BF16)

    proj = _dot(u_ref[0], w_ref[...])
    logits = [lbl_ref[l, 0] for l in range(lbl_ref.shape[0])]
    mx = functools.reduce(jnp.maximum, logits)
    ex = [jnp.exp(l - mx) for l in logits]
    lb = ex[0] / functools.reduce(jnp.add, ex)

    ts = proj.shape[0]
    state_t = state_ref[...]
    for c0 in range(0, ts, chunk):
        p = proj[c0:c0 + chunk]
        q = p[:, 0 * HEAD_DIM:1 * HEAD_DIM]
        fl = p[:, 1 * HEAD_DIM:2 * HEAD_DIM]
        v = p[:, 2 * HEAD_DIM:3 * HEAD_DIM]
        g = p[:, 3 * HEAD_DIM:4 * HEAD_DIM]
        o, state_t = _hgrn_chunk(q, fl, v, lb, state_t)
        o = _rms(o, gn_ref[...]) * (g * jax.nn.sigmoid(g))
        o_ref[0, c0:c0 + chunk, :] = o.astype(o_ref.dtype)
    state_ref[...] = state_t


def _hgrn(u, w_in, lb_logits, gn, *, heads, ts, chunk):
    bsz, s, d = u.shape
    nl = lb_logits.shape[0]
    w_spec = lambda j: pl.BlockSpec((d, HEAD_DIM), lambda b, h, i: (0, j * heads + h))
    return pl.pallas_call(
        functools.partial(_hgrn_kernel, chunk=chunk),
        grid=(bsz, heads, s // ts),
        in_specs=[
            pl.BlockSpec((1, ts, d), lambda b, h, i: (b, i, 0)),
            w_spec(0), w_spec(1), w_spec(2), w_spec(3),
            pl.BlockSpec((nl, 1, 1, HEAD_DIM), lambda b, h, i: (0, h, 0, 0)),
            pl.BlockSpec((1, HEAD_DIM), lambda b, h, i: (0, 0)),
        ],
        out_specs=pl.BlockSpec((1, ts, HEAD_DIM), lambda b, h, i: (b, i, h)),
        out_shape=jax.ShapeDtypeStruct((bsz, s, heads * HEAD_DIM), BF16),
        scratch_shapes=[pltpu.VMEM((d, 4 * HEAD_DIM), BF16), pltpu.VMEM((HEAD_DIM, HEAD_DIM), F32)],
        compiler_params=pltpu.CompilerParams(
            dimension_semantics=("arbitrary", "arbitrary", "arbitrary"),
            vmem_limit_bytes=VMEM_LIMIT_BYTES),
        name="hgrn",
    )(u, w_in, w_in, w_in, w_in, lb_logits, gn)


def _conv_kernel(u_ref, wb_ref, wc_ref, wv_ref, cw_ref, gn_ref, o_ref, w_ref, halo_ref):
    gw = o_ref.shape[2]

    @pl.when((pl.program_id(1) == 0) & (pl.program_id(2) == 0))
    def _():
        for j, src in enumerate((wb_ref, wc_ref, wv_ref)):
            w_ref[:, j * gw:(j + 1) * gw] = src[...].astype(BF16)

    @pl.when(pl.program_id(2) == 0)
    def _():
        halo_ref[...] = jnp.zeros_like(halo_ref)

    proj = _dot(u_ref[0], w_ref[...])
    ts = proj.shape[0]
    bg = proj[:, 0:gw]
    uc = proj[:, gw:2 * gw] * proj[:, 2 * gw:3 * gw]
    row = lax.broadcasted_iota(jnp.int32, (ts, gw), 0)
    prev1 = halo_ref[1:2, :]
    prev2 = halo_ref[0:1, :]
    u1 = jnp.where(row >= 1, pltpu.roll(uc, 1, axis=0), prev1)
    u2 = jnp.where(row >= 2, pltpu.roll(uc, 2, axis=0), jnp.where(row == 1, prev1, prev2))
    halo_ref[...] = uc[ts - 2:ts, :]
    y = bg * (cw_ref[0:1, :] * u2 + cw_ref[1:2, :] * u1 + cw_ref[2:3, :] * uc)
    for g0 in range(0, gw, HEAD_DIM):
        yg = y[:, g0:g0 + HEAD_DIM]
        o_ref[0, :, g0:g0 + HEAD_DIM] = _rms(yg, gn_ref[:, g0:g0 + HEAD_DIM]).astype(o_ref.dtype)


def _conv(u, w_in, conv_w, gn, *, col0, ts, gw):
    bsz, s, d = u.shape
    cw = conv_w.shape[1]
    w_spec = lambda j: pl.BlockSpec((d, gw), lambda g, b, i: (0, (col0 + j * cw) // gw + g))
    return pl.pallas_call(
        _conv_kernel,
        grid=(cw // gw, bsz, s // ts),
        in_specs=[
            pl.BlockSpec((1, ts, d), lambda g, b, i: (b, i, 0)),
            w_spec(0), w_spec(1), w_spec(2),
            pl.BlockSpec((3, gw), lambda g, b, i: (0, g)),
            pl.BlockSpec((1, gw), lambda g, b, i: (0, g)),
        ],
        out_specs=pl.BlockSpec((1, ts, gw), lambda g, b, i: (b, i, g)),
        out_shape=jax.ShapeDtypeStruct((bsz, s, cw), BF16),
        scratch_shapes=[pltpu.VMEM((d, 3 * gw), BF16), pltpu.VMEM((2, gw), F32)],
        compiler_params=pltpu.CompilerParams(
            dimension_semantics=("arbitrary", "arbitrary", "arbitrary"),
            vmem_limit_bytes=VMEM_LIMIT_BYTES),
        name="conv",
    )(u, w_in, w_in, w_in, conv_w, gn)


def _outproj_kernel(h_ref, a_ref, b_ref, wo_ref, o_ref, w_ref):
    @pl.when(pl.program_id(0) == 0)
    def _():
        w_ref[...] = wo_ref[...].astype(BF16)

    ka = a_ref.shape[1]
    o_ref[...] = h_ref[...] + _dot(a_ref[...], w_ref[0:ka, :]) + _dot(b_ref[...], w_ref[ka:, :])


def _outproj(h, a, b, wo, *, tm):
    t, d = h.shape
    ka, kb = a.shape[1], b.shape[1]
    return pl.pallas_call(
        _outproj_kernel,
        grid=(t // tm,),
        in_specs=[
            pl.BlockSpec((tm, d), lambda i: (i, 0)),
            pl.BlockSpec((tm, ka), lambda i: (i, 0)),
            pl.BlockSpec((tm, kb), lambda i: (i, 0)),
            _resident((ka + kb, d), lambda i: (0, 0)),
        ],
        out_specs=pl.BlockSpec((tm, d), lambda i: (i, 0)),
        out_shape=jax.ShapeDtypeStruct((t, d), F32),
        scratch_shapes=[pltpu.VMEM((ka + kb, d), BF16)],
        compiler_params=pltpu.CompilerParams(
            dimension_semantics=("arbitrary",),
            vmem_limit_bytes=VMEM_LIMIT_BYTES),
        name="outproj",
    )(h, a, b, wo)


def _ple_kernel(h_ref, n_ref, p_ref, wg_ref, we_ref, gf_ref, o_ref, w_ref):
    @pl.when(pl.program_id(0) == 0)
    def _():
        w_ref[...] = wg_ref[...].astype(BF16)

    gate = jax.nn.sigmoid(_dot(n_ref[...], w_ref[...]))
    pe = _dot(p_ref[...].astype(BF16), we_ref[...].astype(BF16))
    o_ref[...] = _rms(h_ref[...] + gate * pe, gf_ref[...])


def _ple(h, n, p, wg, we, gf, *, tm):
    t, d = h.shape
    pd = p.shape[1]
    return pl.pallas_call(
        _ple_kernel,
        grid=(t // tm,),
        in_specs=[
            pl.BlockSpec((tm, d), lambda i: (i, 0)),
            pl.BlockSpec((tm, d), lambda i: (i, 0)),
            pl.BlockSpec((tm, pd), lambda i: (i, 0)),
            _resident((d, d), lambda i: (0, 0)),
            _resident((pd, d), lambda i: (0, 0)),
            pl.BlockSpec((1, d), lambda i: (0, 0)),
        ],
        out_specs=pl.BlockSpec((tm, d), lambda i: (i, 0)),
        out_shape=jax.ShapeDtypeStruct((t, d), F32),
        scratch_shapes=[pltpu.VMEM((d, d), BF16)],
        compiler_params=pltpu.CompilerParams(
            dimension_semantics=("arbitrary",),
            vmem_limit_bytes=VMEM_LIMIT_BYTES),
        name="ple",
    )(h, n, p, wg, we, gf)


def kernel(x, p, norm_ffn1, ffn1_gate, ffn1_up, ffn1_down, norm_mix, w_in, conv_w, hgrn_lb_logits,
           hgrn_norm, conv_norm, w_out, norm_ffn2, ffn2_gate, ffn2_up, ffn2_down, norm_ple, w_ple,
           w_ple_gate, norm_final):
    bsz, s, d = x.shape
    depth = w_in.shape[0]
    assert depth == 1, "single-layer block"
    t = bsz * s
    hw = hgrn_lb_logits.shape[1]
    heads = hw // HEAD_DIM
    cw = conv_w.shape[2]
    assert w_in.shape[2] == 4 * hw + 3 * cw

    row = lambda a: a.reshape(1, -1)
    i = 0
    lb_logits = hgrn_lb_logits.reshape(-1, heads, 1, HEAD_DIM)

    h1, u = _ffn(x.reshape(t, d), row(norm_ffn1[i]), ffn1_gate[i], ffn1_up[i], ffn1_down[i],
                 row(norm_mix[i]), tm=1024, tf=256)
    u3 = u.reshape(bsz, s, d)
    o_h = _hgrn(u3, w_in[i], lb_logits, row(hgrn_norm[i]), heads=heads, ts=512, chunk=128)
    y_c = _conv(u3, w_in[i], conv_w[i], row(conv_norm[i]), col0=4 * hw, ts=512, gw=256)
    h2 = _outproj(h1, o_h.reshape(t, hw), y_c.reshape(t, cw), w_out[i], tm=512)
    h3, n3 = _ffn(h2, row(norm_ffn2[i]), ffn2_gate[i], ffn2_up[i], ffn2_down[i],
                  row(norm_ple[i]), tm=1024, tf=256)
    y = _ple(h3, n3, p[i].reshape(t, -1), w_ple_gate[i], w_ple[i], row(norm_final), tm=512)
    return y.reshape(bsz, s, d)
```

```python
import functools

import jax
import jax.numpy as jnp
from jax import lax
from jax.experimental import pallas as pl
from jax.experimental.pallas import tpu as pltpu

EPS = 1e-6
HEAD_DIM = 128
SUBLANES = 8
MXU_COLS = 256
VMEM_LIMIT_BYTES = 56 * 1024 * 1024
FFN_VMEM_LIMIT_BYTES = 60 * 1024 * 1024

F32 = jnp.float32
BF16 = jnp.bfloat16


def _rms(x, gain):
    r = lax.rsqrt(jnp.mean(x * x, axis=-1, keepdims=True) + EPS)
    return (x * r) * gain


def _dot(a, b):
    return jnp.dot(a, b, preferred_element_type=F32)


def _dot_nt(a, b):
    return lax.dot_general(a, b, (((1,), (1,)), ((), ())), preferred_element_type=F32)


def _dot_tn(a, b):
    return lax.dot_general(a, b, (((0,), (0,)), ((), ())), preferred_element_type=F32)


def _resident(shape, index_map):
    return pl.BlockSpec(shape, index_map, pipeline_mode=pl.Buffered(1))


def _ffn_kernel(x_ref, g_ref, wg_ref, wu_ref, wd_ref, gnext_ref, h_ref, n_ref, xn_ref):
    f = pl.program_id(1)

    @pl.when(f == 0)
    def _():
        x = x_ref[...]
        xn_ref[...] = _rms(x, g_ref[...]).astype(BF16)
        h_ref[...] = x

    xn = xn_ref[...]
    gate = _dot(xn, wg_ref[...].astype(BF16))
    up = _dot(xn, wu_ref[...].astype(BF16))
    act = ((0.5 * gate) * jax.nn.sigmoid(gate) * up).astype(BF16)
    h_ref[...] += _dot(act, wd_ref[...].astype(BF16))

    @pl.when(f == pl.num_programs(1) - 1)
    def _():
        n_ref[...] = _rms(h_ref[...], gnext_ref[...]).astype(BF16)


def _ffn(x, gain, wg, wu, wd, gain_next, *, tm, tf):
    t, d = x.shape
    dff = wg.shape[1]
    return pl.pallas_call(
        _ffn_kernel,
        grid=(t // tm, dff // tf),
        in_specs=[
            _resident((tm, d), lambda i, f: (i, 0)),
            pl.BlockSpec((1, d), lambda i, f: (0, 0)),
            pl.BlockSpec((d, tf), lambda i, f: (0, f)),
            pl.BlockSpec((d, tf), lambda i, f: (0, f)),
            pl.BlockSpec((tf, d), lambda i, f: (f, 0)),
            pl.BlockSpec((1, d), lambda i, f: (0, 0)),
        ],
        out_specs=[
            pl.BlockSpec((tm, d), lambda i, f: (i, 0)),
            pl.BlockSpec((tm, d), lambda i, f: (i, 0)),
        ],
        out_shape=[
            jax.ShapeDtypeStruct((t, d), F32),
            jax.ShapeDtypeStruct((t, d), BF16),
        ],
        scratch_shapes=[pltpu.VMEM((tm, d), BF16)],
        compiler_params=pltpu.CompilerParams(
            dimension_semantics=("arbitrary", "arbitrary"),
            vmem_limit_bytes=FFN_VMEM_LIMIT_BYTES),
        name="ffn",
    )(x, gain, wg, wu, wd, gain_next)


def _bcast_row(x, blk, r):
    c, l = x.shape
    x3 = x.reshape(c // blk, blk, l)
    return jnp.broadcast_to(x3[:, r:r + 1, :], (c // blk, blk, l)).reshape(c, l)


def _chunk_cumsum(x, tri):
    c, w = tri.shape[0], x.shape[1]
    hi = x.astype(BF16)
    r1 = x - hi.astype(F32)
    mid = r1.astype(BF16)
    lo = (r1 - mid.astype(F32)).astype(BF16)
    parts = jnp.concatenate([hi, mid, lo], axis=1)
    cs = jnp.concatenate([_dot(tri, parts[c0:c0 + c]) for c0 in range(0, x.shape[0], c)], axis=0)
    return cs[:, 0:w] + cs[:, w:2 * w] + cs[:, 2 * w:3 * w]


def _split_distance(b2, l2, m, sub):
    c = b2.shape[0]
    if m == 2:
        return jnp.where(sub % 2 == 1, l2, 0.0)
    if m == 4:
        return b2 - jnp.where(sub < 4, _bcast_row(b2, SUBLANES, 1), _bcast_row(b2, SUBLANES, 5))
    if m == SUBLANES:
        return b2 - _bcast_row(b2, SUBLANES, 3)
    half = m // 2
    return jnp.concatenate(
        [b2[r0:r0 + m] - b2[r0 + half - 1:r0 + half] for r0 in range(0, c, m)], axis=0)


def _hgrn_tile(p_ref, lb, gn, state_t, o_ref, chunk, side_work):
    ts = p_ref.shape[0]
    starts = range(0, ts, chunk)
    rows = lambda x, c0: x[c0:c0 + chunk]
    q = p_ref[:, 0 * HEAD_DIM:1 * HEAD_DIM]
    fl = p_ref[:, 1 * HEAD_DIM:2 * HEAD_DIM]
    v = p_ref[:, 2 * HEAD_DIM:3 * HEAD_DIM].astype(BF16)
    g = p_ref[:, 3 * HEAD_DIM:4 * HEAD_DIM]

    tt = lax.broadcasted_iota(jnp.int32, (chunk, chunk), 0)
    ss = lax.broadcasted_iota(jnp.int32, (chunk, chunk), 1)
    tri = (ss <= tt).astype(BF16)
    diag = chunk.bit_length() - 1
    level = jnp.where(ss < tt, 31 - lax.clz(tt ^ ss), jnp.where(ss == tt, diag, -1))

    n_slots = diag + 2

    def side(slot):
        for p, work in enumerate(side_work):
            if p * n_slots // len(side_work) == slot:
                work()

    f = lb + (1.0 - lb) * jax.nn.sigmoid(fl)
    k = 1.0 - f
    qf = q * jax.nn.sigmoid(q)
    l2 = jnp.log2(f)
    side(0)
    b2 = _chunk_cumsum(l2, tri)

    qs = (qf * jnp.exp2(b2)).astype(BF16)
    to_end = jnp.concatenate([b2[c0 + chunk - 1:c0 + chunk] - rows(b2, c0) for c0 in starts], axis=0)
    kd = (k * jnp.exp2(to_end)).astype(BF16)
    carried = []
    for c0 in starts:
        carried.append(_dot_nt(rows(qs, c0), state_t.astype(BF16)))
        state_t = state_t * jnp.exp2(b2[c0 + chunk - 1:c0 + chunk]) + _dot_tn(rows(v, c0), rows(kd, c0))

    sub = lax.broadcasted_iota(jnp.int32, (ts, HEAD_DIM), 0) % SUBLANES
    qb, kb = qf.astype(BF16), k.astype(BF16)
    side(1)
    scores = [jnp.where(level == diag, _dot_nt(rows(qb, c0), rows(kb, c0)), 0.0) for c0 in starts]
    m, li = 2, 0
    while m <= chunk:
        e = jnp.exp2(-jnp.abs(_split_distance(b2, l2, m, sub)))
        qe, ke = (qf * e).astype(BF16), (k * e).astype(BF16)
        side(2 + li)
        scores = [jnp.where(level == li, _dot_nt(rows(qe, c0), rows(ke, c0)), sc)
                  for c0, sc in zip(starts, scores)]
        m, li = 2 * m, li + 1

    o = jnp.concatenate([_dot(sc.astype(BF16), rows(v, c0)) + oc for c0, sc, oc in zip(starts, scores, carried)],
                        axis=0)
    o_ref[0] = (_rms(o, gn) * (g * jax.nn.sigmoid(g))).astype(o_ref.dtype)
    return state_t


def _hgrn_kernel(u_ref, wq_ref, wf_ref, wi_ref, wg_ref, lbl_ref, gn_ref, o_ref,
                 w_ref, pa_ref, pb_ref, state_ref, *, chunk, piece_rows, ni, n):
    j = pl.program_id(0)
    ts = pa_ref.shape[0]

    @pl.when(j == 0)
    def _():
        pb_ref[...] = jnp.zeros_like(pb_ref)
        state_ref[...] = jnp.zeros_like(state_ref)

    @pl.when((j % ni == 0) & (j < n))
    def _():
        for sec, src in enumerate((wq_ref, wf_ref, wi_ref, wg_ref)):
            w_ref[:, sec * HEAD_DIM:(sec + 1) * HEAD_DIM] = src[...].astype(BF16)

    def step(p_new_ref, p_ref):
        def piece(r0, c0):
            p_new_ref[r0:r0 + piece_rows, c0:c0 + MXU_COLS] = _dot(
                u_ref[0, r0:r0 + piece_rows, :], w_ref[:, c0:c0 + MXU_COLS])

        pieces = [functools.partial(piece, r0, c0)
                  for r0 in range(0, ts, piece_rows) for c0 in range(0, 4 * HEAD_DIM, MXU_COLS)]

        logits = [lbl_ref[l, 0] for l in range(lbl_ref.shape[0])]
        mx = functools.reduce(jnp.maximum, logits)
        ex = [jnp.exp(l - mx) for l in logits]
        lb = ex[0] / functools.reduce(jnp.add, ex)

        first = (jnp.maximum(j - 1, 0) % ni) == 0
        state_t = jnp.where(first, 0.0, state_ref[...])
        state_ref[...] = _hgrn_tile(p_ref, lb, gn_ref[...], state_t, o_ref, chunk, pieces)

    @pl.when(j % 2 == 0)
    def _():
        step(pa_ref, pb_ref)

    @pl.when(j % 2 == 1)
    def _():
        step(pb_ref, pa_ref)


def _hgrn(u, w_in, lb_logits, gn, *, heads, ts, chunk, piece_rows):
    bsz, s, d = u.shape
    nl = lb_logits.shape[0]
    ni = s // ts
    n = bsz * heads * ni

    def cur(j):
        jc = jnp.minimum(j, n - 1)
        return jc // (heads * ni), (jc // ni) % heads, jc % ni

    def prev(j):
        return cur(jnp.maximum(j - 1, 0))

    w_spec = lambda sec: pl.BlockSpec((d, HEAD_DIM), lambda j: (0, sec * heads + cur(j)[1]))
    return pl.pallas_call(
        functools.partial(_hgrn_kernel, chunk=chunk, piece_rows=piece_rows, ni=ni, n=n),
        grid=(n + 1,),
        in_specs=[
            pl.BlockSpec((1, ts, d), lambda j: (cur(j)[0], cur(j)[2], 0)),
            w_spec(0), w_spec(1), w_spec(2), w_spec(3),
            pl.BlockSpec((nl, 1, 1, HEAD_DIM), lambda j: (0, prev(j)[1], 0, 0)),
            pl.BlockSpec((1, HEAD_DIM), lambda j: (0, 0)),
        ],
        out_specs=pl.BlockSpec((1, ts, HEAD_DIM), lambda j: (prev(j)[0], prev(j)[2], prev(j)[1])),
        out_shape=jax.ShapeDtypeStruct((bsz, s, heads * HEAD_DIM), BF16),
        scratch_shapes=[
            pltpu.VMEM((d, 4 * HEAD_DIM), BF16),
            pltpu.VMEM((ts, 4 * HEAD_DIM), F32),
            pltpu.VMEM((ts, 4 * HEAD_DIM), F32),
            pltpu.VMEM((HEAD_DIM, HEAD_DIM), F32),
        ],
        compiler_params=pltpu.CompilerParams(
            dimension_semantics=("arbitrary",),
            vmem_limit_bytes=VMEM_LIMIT_BYTES),
        name="hgrn",
    )(u, w_in, w_in, w_in, w_in, lb_logits, gn)


def _conv_kernel(u_ref, wb_ref, wc_ref, wv_ref, cw_ref, gn_ref, o_ref,
                 w_ref, pa_ref, pb_ref, halo_ref, *, ni, sweep, n):
    j = pl.program_id(0)
    ts, gw = o_ref.shape[1], o_ref.shape[2]

    @pl.when(j == 0)
    def _():
        pb_ref[...] = jnp.zeros_like(pb_ref)
        halo_ref[...] = jnp.zeros_like(halo_ref)

    @pl.when((j % sweep == 0) & (j < n))
    def _():
        for sec, src in enumerate((wb_ref, wc_ref, wv_ref)):
            w_ref[:, sec * gw:(sec + 1) * gw] = src[...].astype(BF16)

    def step(p_new_ref, p_ref):
        p_new_ref[...] = _dot(u_ref[0], w_ref[...])

        first = (jnp.maximum(j - 1, 0) % ni) == 0
        bg = p_ref[:, 0:gw]
        uc = p_ref[:, gw:2 * gw] * p_ref[:, 2 * gw:3 * gw]
        row = lax.broadcasted_iota(jnp.int32, (ts, gw), 0)
        prev1 = jnp.where(first, 0.0, halo_ref[1:2, :])
        prev2 = jnp.where(first, 0.0, halo_ref[0:1, :])
        u1 = jnp.where(row >= 1, pltpu.roll(uc, 1, axis=0), prev1)
        u2 = jnp.where(row >= 2, pltpu.roll(uc, 2, axis=0), jnp.where(row == 1, prev1, prev2))
        halo_ref[...] = uc[ts - 2:ts, :]
        y = bg * (cw_ref[0:1, :] * u2 + cw_ref[1:2, :] * u1 + cw_ref[2:3, :] * uc)
        for g0 in range(0, gw, HEAD_DIM):
            yg = y[:, g0:g0 + HEAD_DIM]
            o_ref[0, :, g0:g0 + HEAD_DIM] = _rms(yg, gn_ref[:, g0:g0 + HEAD_DIM]).astype(o_ref.dtype)

    @pl.when(j % 2 == 0)
    def _():
        step(pa_ref, pb_ref)

    @pl.when(j % 2 == 1)
    def _():
        step(pb_ref, pa_ref)


def _conv(u, w_in, conv_w, gn, *, col0, ts, gw):
    bsz, s, d = u.shape
    cw = conv_w.shape[1]
    ni = s // ts
    sweep = bsz * ni
    n = (cw // gw) * sweep

    def cur(j):
        jc = jnp.minimum(j, n - 1)
        return jc // sweep, (jc // ni) % bsz, jc % ni

    def prev(j):
        return cur(jnp.maximum(j - 1, 0))

    w_spec = lambda sec: pl.BlockSpec((d, gw), lambda j: (0, (col0 + sec * cw) // gw + cur(j)[0]))
    return pl.pallas_call(
        functools.partial(_conv_kernel, ni=ni, sweep=sweep, n=n),
        grid=(n + 1,),
        in_specs=[
            pl.BlockSpec((1, ts, d), lambda j: (cur(j)[1], cur(j)[2], 0)),
            w_spec(0), w_spec(1), w_spec(2),
            pl.BlockSpec((3, gw), lambda j: (0, prev(j)[0])),
            pl.BlockSpec((1, gw), lambda j: (0, prev(j)[0])),
        ],
        out_specs=pl.BlockSpec((1, ts, gw), lambda j: (prev(j)[1], prev(j)[2], prev(j)[0])),
        out_shape=jax.ShapeDtypeStruct((bsz, s, cw), BF16),
        scratch_shapes=[
            pltpu.VMEM((d, 3 * gw), BF16),
            pltpu.VMEM((ts, 3 * gw), F32),
            pltpu.VMEM((ts, 3 * gw), F32),
            pltpu.VMEM((2, gw), F32),
        ],
        compiler_params=pltpu.CompilerParams(
            dimension_semantics=("arbitrary",),
            vmem_limit_bytes=VMEM_LIMIT_BYTES),
        name="conv",
    )(u, w_in, w_in, w_in, conv_w, gn)


def _outproj_kernel(h_ref, a_ref, b_ref, wo_ref, o_ref, w_ref):
    @pl.when(pl.program_id(0) == 0)
    def _():
        w_ref[...] = wo_ref[...].astype(BF16)

    ka = a_ref.shape[1]
    o_ref[...] = h_ref[...] + _dot(a_ref[...], w_ref[0:ka, :]) + _dot(b_ref[...], w_ref[ka:, :])


def _outproj(h, a, b, wo, *, tm):
    t, d = h.shape
    ka, kb = a.shape[1], b.shape[1]
    return pl.pallas_call(
        _outproj_kernel,
        grid=(t // tm,),
        in_specs=[
            pl.BlockSpec((tm, d), lambda i: (i, 0)),
            pl.BlockSpec((tm, ka), lambda i: (i, 0)),
            pl.BlockSpec((tm, kb), lambda i: (i, 0)),
            _resident((ka + kb, d), lambda i: (0, 0)),
        ],
        out_specs=pl.BlockSpec((tm, d), lambda i: (i, 0)),
        out_shape=jax.ShapeDtypeStruct((t, d), F32),
        scratch_shapes=[pltpu.VMEM((ka + kb, d), BF16)],
        compiler_params=pltpu.CompilerParams(
            dimension_semantics=("arbitrary",),
            vmem_limit_bytes=VMEM_LIMIT_BYTES),
        name="outproj",
    )(h, a, b, wo)


def _ple_kernel(h_ref, n_ref, p_ref, wg_ref, we_ref, gf_ref, o_ref, w_ref):
    @pl.when(pl.program_id(0) == 0)
    def _():
        w_ref[...] = wg_ref[...].astype(BF16)

    gate = jax.nn.sigmoid(_dot(n_ref[...], w_ref[...]))
    pe = _dot(p_ref[...].astype(BF16), we_ref[...].astype(BF16))
    o_ref[...] = _rms(h_ref[...] + gate * pe, gf_ref[...])


def _ple(h, n, p, wg, we, gf, *, tm):
    t, d = h.shape
    pd = p.shape[1]
    return pl.pallas_call(
        _ple_kernel,
        grid=(t // tm,),
        in_specs=[
            pl.BlockSpec((tm, d), lambda i: (i, 0)),
            pl.BlockSpec((tm, d), lambda i: (i, 0)),
            pl.BlockSpec((tm, pd), lambda i: (i, 0)),
            _resident((d, d), lambda i: (0, 0)),
            _resident((pd, d), lambda i: (0, 0)),
            pl.BlockSpec((1, d), lambda i: (0, 0)),
        ],
        out_specs=pl.BlockSpec((tm, d), lambda i: (i, 0)),
        out_shape=jax.ShapeDtypeStruct((t, d), F32),
        scratch_shapes=[pltpu.VMEM((d, d), BF16)],
        compiler_params=pltpu.CompilerParams(
            dimension_semantics=("arbitrary",),
            vmem_limit_bytes=VMEM_LIMIT_BYTES),
        name="ple",
    )(h, n, p, wg, we, gf)


def kernel(x, p, norm_ffn1, ffn1_gate, ffn1_up, ffn1_down, norm_mix, w_in, conv_w, hgrn_lb_logits,
           hgrn_norm, conv_norm, w_out, norm_ffn2, ffn2_gate, ffn2_up, ffn2_down, norm_ple, w_ple,
           w_ple_gate, norm_final):
    bsz, s, d = x.shape
    depth = w_in.shape[0]
    assert depth == 1, "single-layer block"
    t = bsz * s
    hw = hgrn_lb_logits.shape[1]
    heads = hw // HEAD_DIM
    cw = conv_w.shape[2]
    assert w_in.shape[2] == 4 * hw + 3 * cw

    row = lambda a: a.reshape(1, -1)
    i = 0
    lb_logits = hgrn_lb_logits.reshape(-1, heads, 1, HEAD_DIM)

    h1, u = _ffn(x.reshape(t, d), row(norm_ffn1[i]), ffn1_gate[i], ffn1_up[i], ffn1_down[i],
                 row(norm_mix[i]), tm=1024, tf=256)
    u3 = u.reshape(bsz, s, d)
    o_h = _hgrn(u3, w_in[i], lb_logits, row(hgrn_norm[i]), heads=heads, ts=512, chunk=128,
              piece_rows=128)
    y_c = _conv(u3, w_in[i], conv_w[i], row(conv_norm[i]), col0=4 * hw, ts=512, gw=256)
    h2 = _outproj(h1, o_h.reshape(t, hw), y_c.reshape(t, cw), w_out[i], tm=512)
    h3, n3 = _ffn(h2, row(norm_ffn2[i]), ffn2_gate[i], ffn2_up[i], ffn2_down[i],
                  row(norm_ple[i]), tm=1024, tf=256)
    y = _ple(h3, n3, p[i].reshape(t, -1), w_ple_gate[i], w_ple[i], row(norm_final), tm=512)
    return y.reshape(bsz, s, d)
```

```python
import functools

import jax
import jax.numpy as jnp
from jax import lax
from jax.experimental import pallas as pl
from jax.experimental.pallas import tpu as pltpu

EPS = 1e-6
HEAD_DIM = 128
SUBLANES = 8
MXU_COLS = 256
VMEM_LIMIT_BYTES = 56 * 1024 * 1024
FFN_VMEM_LIMIT_BYTES = 60 * 1024 * 1024

F32 = jnp.float32
BF16 = jnp.bfloat16


def _rms(x, gain):
    r = lax.rsqrt(jnp.mean(x * x, axis=-1, keepdims=True) + EPS)
    return (x * r) * gain


def _dot(a, b):
    return jnp.dot(a, b, preferred_element_type=F32)


def _dot_nt(a, b):
    return lax.dot_general(a, b, (((1,), (1,)), ((), ())), preferred_element_type=F32)


def _dot_tn(a, b):
    return lax.dot_general(a, b, (((0,), (0,)), ((), ())), preferred_element_type=F32)


def _resident(shape, index_map):
    return pl.BlockSpec(shape, index_map, pipeline_mode=pl.Buffered(1))


def _ffn_kernel(x_hbm, g_ref, wg_ref, wu_ref, wd_ref, gnext_ref, h_hbm, n_ref,
                acc_ref, xn_ref, in_sem, out_sem):
    i, f = pl.program_id(0), pl.program_id(1)
    nt, nf = pl.num_programs(0), pl.num_programs(1)
    tm = xn_ref.shape[0]
    slot = i % 2

    def x_copy(tile, s):
        return pltpu.make_async_copy(x_hbm.at[pl.ds(tile * tm, tm)], acc_ref.at[s], in_sem.at[s])

    def h_copy(tile, s):
        return pltpu.make_async_copy(acc_ref.at[s], h_hbm.at[pl.ds(tile * tm, tm)], out_sem.at[s])

    @pl.when(f == 0)
    def _():
        @pl.when(i == 0)
        def _():
            x_copy(0, 0).start()

        x_copy(i, slot).wait()
        xn_ref[...] = _rms(acc_ref[slot], g_ref[...]).astype(BF16)

    @pl.when(f == 1)
    def _():
        @pl.when(i > 0)
        def _():
            h_copy(i - 1, 1 - slot).wait()

        @pl.when(i + 1 < nt)
        def _():
            x_copy(i + 1, 1 - slot).start()

    xn = xn_ref[...]
    gate = _dot(xn, wg_ref[...].astype(BF16))
    up = _dot(xn, wu_ref[...].astype(BF16))
    act = ((0.5 * gate) * jax.nn.sigmoid(gate) * up).astype(BF16)
    acc_ref[slot] += _dot(act, wd_ref[...].astype(BF16))

    @pl.when(f == nf - 1)
    def _():
        n_ref[...] = _rms(acc_ref[slot], gnext_ref[...]).astype(BF16)
        h_copy(i, slot).start()

        @pl.when(i == nt - 1)
        def _():
            h_copy(i, slot).wait()


def _ffn(x, gain, wg, wu, wd, gain_next, *, tm, tf):
    t, d = x.shape
    dff = wg.shape[1]
    assert dff // tf >= 2, "the accumulator hand-over happens at the second F step"
    return pl.pallas_call(
        _ffn_kernel,
        grid=(t // tm, dff // tf),
        in_specs=[
            pl.BlockSpec(memory_space=pl.ANY),
            pl.BlockSpec((1, d), lambda i, f: (0, 0)),
            pl.BlockSpec((d, tf), lambda i, f: (0, f)),
            pl.BlockSpec((d, tf), lambda i, f: (0, f)),
            pl.BlockSpec((tf, d), lambda i, f: (f, 0)),
            pl.BlockSpec((1, d), lambda i, f: (0, 0)),
        ],
        out_specs=[
            pl.BlockSpec(memory_space=pl.ANY),
            pl.BlockSpec((tm, d), lambda i, f: (i, 0)),
        ],
        out_shape=[
            jax.ShapeDtypeStruct((t, d), F32),
            jax.ShapeDtypeStruct((t, d), BF16),
        ],
        scratch_shapes=[
            pltpu.VMEM((2, tm, d), F32),
            pltpu.VMEM((tm, d), BF16),
            pltpu.SemaphoreType.DMA((2,)),
            pltpu.SemaphoreType.DMA((2,)),
        ],
        compiler_params=pltpu.CompilerParams(
            dimension_semantics=("arbitrary", "arbitrary"),
            vmem_limit_bytes=FFN_VMEM_LIMIT_BYTES),
        name="ffn",
    )(x, gain, wg, wu, wd, gain_next)


def _bcast_row(x, blk, r):
    c, l = x.shape
    x3 = x.reshape(c // blk, blk, l)
    return jnp.broadcast_to(x3[:, r:r + 1, :], (c // blk, blk, l)).reshape(c, l)


def _chunk_cumsum(x, tri):
    c, w = tri.shape[0], x.shape[1]
    hi = x.astype(BF16)
    r1 = x - hi.astype(F32)
    mid = r1.astype(BF16)
    lo = (r1 - mid.astype(F32)).astype(BF16)
    parts = jnp.concatenate([hi, mid, lo], axis=1)
    cs = jnp.concatenate([_dot(tri, parts[c0:c0 + c]) for c0 in range(0, x.shape[0], c)], axis=0)
    return cs[:, 0:w] + cs[:, w:2 * w] + cs[:, 2 * w:3 * w]


def _split_distance(b2, l2, m, sub):
    c = b2.shape[0]
    if m == 2:
        return jnp.where(sub % 2 == 1, l2, 0.0)
    if m == 4:
        return b2 - jnp.where(sub < 4, _bcast_row(b2, SUBLANES, 1), _bcast_row(b2, SUBLANES, 5))
    if m == SUBLANES:
        return b2 - _bcast_row(b2, SUBLANES, 3)
    half = m // 2
    return jnp.concatenate(
        [b2[r0:r0 + m] - b2[r0 + half - 1:r0 + half] for r0 in range(0, c, m)], axis=0)


def _hgrn_tile(p_ref, lb, gn, state_t, o_ref, chunk, side_work):
    ts = p_ref.shape[0]
    starts = range(0, ts, chunk)
    rows = lambda x, c0: x[c0:c0 + chunk]
    q = p_ref[:, 0 * HEAD_DIM:1 * HEAD_DIM]
    fl = p_ref[:, 1 * HEAD_DIM:2 * HEAD_DIM]
    v = p_ref[:, 2 * HEAD_DIM:3 * HEAD_DIM].astype(BF16)
    g = p_ref[:, 3 * HEAD_DIM:4 * HEAD_DIM]

    tt = lax.broadcasted_iota(jnp.int32, (chunk, chunk), 0)
    ss = lax.broadcasted_iota(jnp.int32, (chunk, chunk), 1)
    tri = (ss <= tt).astype(BF16)
    diag = chunk.bit_length() - 1
    level = jnp.where(ss < tt, 31 - lax.clz(tt ^ ss), jnp.where(ss == tt, diag, -1))

    n_slots = diag + 2

    def side(slot):
        for p, work in enumerate(side_work):
            if p * n_slots // len(side_work) == slot:
                work()

    f = lb + (1.0 - lb) * jax.nn.sigmoid(fl)
    k = 1.0 - f
    qf = q * jax.nn.sigmoid(q)
    l2 = jnp.log2(f)
    side(0)
    b2 = _chunk_cumsum(l2, tri)

    qs = (qf * jnp.exp2(b2)).astype(BF16)
    to_end = jnp.concatenate([b2[c0 + chunk - 1:c0 + chunk] - rows(b2, c0) for c0 in starts], axis=0)
    kd = (k * jnp.exp2(to_end)).astype(BF16)
    carried = []
    for c0 in starts:
        carried.append(_dot_nt(rows(qs, c0), state_t.astype(BF16)))
        state_t = state_t * jnp.exp2(b2[c0 + chunk - 1:c0 + chunk]) + _dot_tn(rows(v, c0), rows(kd, c0))

    sub = lax.broadcasted_iota(jnp.int32, (ts, HEAD_DIM), 0) % SUBLANES
    qb, kb = qf.astype(BF16), k.astype(BF16)
    side(1)
    scores = [jnp.where(level == diag, _dot_nt(rows(qb, c0), rows(kb, c0)), 0.0) for c0 in starts]
    m, li = 2, 0
    while m <= chunk:
        e = jnp.exp2(-jnp.abs(_split_distance(b2, l2, m, sub)))
        qe, ke = (qf * e).astype(BF16), (k * e).astype(BF16)
        side(2 + li)
        scores = [jnp.where(level == li, _dot_nt(rows(qe, c0), rows(ke, c0)), sc)
                  for c0, sc in zip(starts, scores)]
        m, li = 2 * m, li + 1

    o = jnp.concatenate([_dot(sc.astype(BF16), rows(v, c0)) + oc for c0, sc, oc in zip(starts, scores, carried)],
                        axis=0)
    o_ref[0] = (_rms(o, gn) * (g * jax.nn.sigmoid(g))).astype(o_ref.dtype)
    return state_t


def _hgrn_kernel(u_ref, wq_ref, wf_ref, wi_ref, wg_ref, lbl_ref, gn_ref, o_ref,
                 w_ref, pa_ref, pb_ref, state_ref, *, chunk, piece_rows, ni, n):
    j = pl.program_id(0)
    ts = pa_ref.shape[0]

    @pl.when(j == 0)
    def _():
        pb_ref[...] = jnp.zeros_like(pb_ref)
        state_ref[...] = jnp.zeros_like(state_ref)

    @pl.when((j % ni == 0) & (j < n))
    def _():
        for sec, src in enumerate((wq_ref, wf_ref, wi_ref, wg_ref)):
            w_ref[:, sec * HEAD_DIM:(sec + 1) * HEAD_DIM] = src[...].astype(BF16)

    def step(p_new_ref, p_ref):
        def piece(r0, c0):
            p_new_ref[r0:r0 + piece_rows, c0:c0 + MXU_COLS] = _dot(
                u_ref[0, r0:r0 + piece_rows, :], w_ref[:, c0:c0 + MXU_COLS])

        pieces = [functools.partial(piece, r0, c0)
                  for r0 in range(0, ts, piece_rows) for c0 in range(0, 4 * HEAD_DIM, MXU_COLS)]

        logits = [lbl_ref[l, 0] for l in range(lbl_ref.shape[0])]
        mx = functools.reduce(jnp.maximum, logits)
        ex = [jnp.exp(l - mx) for l in logits]
        lb = ex[0] / functools.reduce(jnp.add, ex)

        first = (jnp.maximum(j - 1, 0) % ni) == 0
        state_t = jnp.where(first, 0.0, state_ref[...])
        state_ref[...] = _hgrn_tile(p_ref, lb, gn_ref[...], state_t, o_ref, chunk, pieces)

    @pl.when(j % 2 == 0)
    def _():
        step(pa_ref, pb_ref)

    @pl.when(j % 2 == 1)
    def _():
        step(pb_ref, pa_ref)


def _hgrn(u, w_in, lb_logits, gn, *, heads, ts, chunk, piece_rows):
    bsz, s, d = u.shape
    nl = lb_logits.shape[0]
    ni = s // ts
    n = bsz * heads * ni

    def cur(j):
        jc = jnp.minimum(j, n - 1)
        return jc // (heads * ni), (jc // ni) % heads, jc % ni

    def prev(j):
        return cur(jnp.maximum(j - 1, 0))

    w_spec = lambda sec: pl.BlockSpec((d, HEAD_DIM), lambda j: (0, sec * heads + cur(j)[1]))
    return pl.pallas_call(
        functools.partial(_hgrn_kernel, chunk=chunk, piece_rows=piece_rows, ni=ni, n=n),
        grid=(n + 1,),
        in_specs=[
            pl.BlockSpec((1, ts, d), lambda j: (cur(j)[0], cur(j)[2], 0)),
            w_spec(0), w_spec(1), w_spec(2), w_spec(3),
            pl.BlockSpec((nl, 1, 1, HEAD_DIM), lambda j: (0, prev(j)[1], 0, 0)),
            pl.BlockSpec((1, HEAD_DIM), lambda j: (0, 0)),
        ],
        out_specs=pl.BlockSpec((1, ts, HEAD_DIM), lambda j: (prev(j)[0], prev(j)[2], prev(j)[1])),
        out_shape=jax.ShapeDtypeStruct((bsz, s, heads * HEAD_DIM), BF16),
        scratch_shapes=[
            pltpu.VMEM((d, 4 * HEAD_DIM), BF16),
            pltpu.VMEM((ts, 4 * HEAD_DIM), F32),
            pltpu.VMEM((ts, 4 * HEAD_DIM), F32),
            pltpu.VMEM((HEAD_DIM, HEAD_DIM), F32),
        ],
        compiler_params=pltpu.CompilerParams(
            dimension_semantics=("arbitrary",),
            vmem_limit_bytes=VMEM_LIMIT_BYTES),
        name="hgrn",
    )(u, w_in, w_in, w_in, w_in, lb_logits, gn)


def _conv_kernel(u_ref, wb_ref, wc_ref, wv_ref, cw_ref, gn_ref, o_ref,
                 w_ref, pa_ref, pb_ref, halo_ref, *, ni, sweep, n):
    j = pl.program_id(0)
    ts, gw = o_ref.shape[1], o_ref.shape[2]

    @pl.when(j == 0)
    def _():
        pb_ref[...] = jnp.zeros_like(pb_ref)
        halo_ref[...] = jnp.zeros_like(halo_ref)

    @pl.when((j % sweep == 0) & (j < n))
    def _():
        for sec, src in enumerate((wb_ref, wc_ref, wv_ref)):
            w_ref[:, sec * gw:(sec + 1) * gw] = src[...].astype(BF16)

    def step(p_new_ref, p_ref):
        p_new_ref[...] = _dot(u_ref[0], w_ref[...])

        first = (jnp.maximum(j - 1, 0) % ni) == 0
        bg = p_ref[:, 0:gw]
        uc = p_ref[:, gw:2 * gw] * p_ref[:, 2 * gw:3 * gw]
        row = lax.broadcasted_iota(jnp.int32, (ts, gw), 0)
        prev1 = jnp.where(first, 0.0, halo_ref[1:2, :])
        prev2 = jnp.where(first, 0.0, halo_ref[0:1, :])
        u1 = jnp.where(row >= 1, pltpu.roll(uc, 1, axis=0), prev1)
        u2 = jnp.where(row >= 2, pltpu.roll(uc, 2, axis=0), jnp.where(row == 1, prev1, prev2))
        halo_ref[...] = uc[ts - 2:ts, :]
        y = bg * (cw_ref[0:1, :] * u2 + cw_ref[1:2, :] * u1 + cw_ref[2:3, :] * uc)
        for g0 in range(0, gw, HEAD_DIM):
            yg = y[:, g0:g0 + HEAD_DIM]
            o_ref[0, :, g0:g0 + HEAD_DIM] = _rms(yg, gn_ref[:, g0:g0 + HEAD_DIM]).astype(o_ref.dtype)

    @pl.when(j % 2 == 0)
    def _():
        step(pa_ref, pb_ref)

    @pl.when(j % 2 == 1)
    def _():
        step(pb_ref, pa_ref)


def _conv(u, w_in, conv_w, gn, *, col0, ts, gw):
    bsz, s, d = u.shape
    cw = conv_w.shape[1]
    ni = s // ts
    sweep = bsz * ni
    n = (cw // gw) * sweep

    def cur(j):
        jc = jnp.minimum(j, n - 1)
        return jc // sweep, (jc // ni) % bsz, jc % ni

    def prev(j):
        return cur(jnp.maximum(j - 1, 0))

    w_spec = lambda sec: pl.BlockSpec((d, gw), lambda j: (0, (col0 + sec * cw) // gw + cur(j)[0]))
    return pl.pallas_call(
        functools.partial(_conv_kernel, ni=ni, sweep=sweep, n=n),
        grid=(n + 1,),
        in_specs=[
            pl.BlockSpec((1, ts, d), lambda j: (cur(j)[1], cur(j)[2], 0)),
            w_spec(0), w_spec(1), w_spec(2),
            pl.BlockSpec((3, gw), lambda j: (0, prev(j)[0])),
            pl.BlockSpec((1, gw), lambda j: (0, prev(j)[0])),
        ],
        out_specs=pl.BlockSpec((1, ts, gw), lambda j: (prev(j)[1], prev(j)[2], prev(j)[0])),
        out_shape=jax.ShapeDtypeStruct((bsz, s, cw), BF16),
        scratch_shapes=[
            pltpu.VMEM((d, 3 * gw), BF16),
            pltpu.VMEM((ts, 3 * gw), F32),
            pltpu.VMEM((ts, 3 * gw), F32),
            pltpu.VMEM((2, gw), F32),
        ],
        compiler_params=pltpu.CompilerParams(
            dimension_semantics=("arbitrary",),
            vmem_limit_bytes=VMEM_LIMIT_BYTES),
        name="conv",
    )(u, w_in, w_in, w_in, conv_w, gn)


def _outproj_kernel(h_ref, a_ref, b_ref, wo_ref, o_ref, w_ref):
    @pl.when(pl.program_id(0) == 0)
    def _():
        w_ref[...] = wo_ref[...].astype(BF16)

    ka = a_ref.shape[1]
    o_ref[...] = h_ref[...] + _dot(a_ref[...], w_ref[0:ka, :]) + _dot(b_ref[...], w_ref[ka:, :])


def _outproj(h, a, b, wo, *, tm):
    t, d = h.shape
    ka, kb = a.shape[1], b.shape[1]
    return pl.pallas_call(
        _outproj_kernel,
        grid=(t // tm,),
        in_specs=[
            pl.BlockSpec((tm, d), lambda i: (i, 0)),
            pl.BlockSpec((tm, ka), lambda i: (i, 0)),
            pl.BlockSpec((tm, kb), lambda i: (i, 0)),
            _resident((ka + kb, d), lambda i: (0, 0)),
        ],
        out_specs=pl.BlockSpec((tm, d), lambda i: (i, 0)),
        out_shape=jax.ShapeDtypeStruct((t, d), F32),
        scratch_shapes=[pltpu.VMEM((ka + kb, d), BF16)],
        compiler_params=pltpu.CompilerParams(
            dimension_semantics=("arbitrary",),
            vmem_limit_bytes=VMEM_LIMIT_BYTES),
        name="outproj",
    )(h, a, b, wo)


def _ple_kernel(h_ref, n_ref, p_ref, wg_ref, we_ref, gf_ref, o_ref, w_ref):
    @pl.when(pl.program_id(0) == 0)
    def _():
        w_ref[...] = wg_ref[...].astype(BF16)

    gate = jax.nn.sigmoid(_dot(n_ref[...], w_ref[...]))
    pe = _dot(p_ref[...].astype(BF16), we_ref[...].astype(BF16))
    o_ref[...] = _rms(h_ref[...] + gate * pe, gf_ref[...])


def _ple(h, n, p, wg, we, gf, *, tm):
    t, d = h.shape
    pd = p.shape[1]
    return pl.pallas_call(
        _ple_kernel,
        grid=(t // tm,),
        in_specs=[
            pl.BlockSpec((tm, d), lambda i: (i, 0)),
            pl.BlockSpec((tm, d), lambda i: (i, 0)),
            pl.BlockSpec((tm, pd), lambda i: (i, 0)),
            _resident((d, d), lambda i: (0, 0)),
            _resident((pd, d), lambda i: (0, 0)),
            pl.BlockSpec((1, d), lambda i: (0, 0)),
        ],
        out_specs=pl.BlockSpec((tm, d), lambda i: (i, 0)),
        out_shape=jax.ShapeDtypeStruct((t, d), F32),
        scratch_shapes=[pltpu.VMEM((d, d), BF16)],
        compiler_params=pltpu.CompilerParams(
            dimension_semantics=("arbitrary",),
            vmem_limit_bytes=VMEM_LIMIT_BYTES),
        name="ple",
    )(h, n, p, wg, we, gf)


def kernel(x, p, norm_ffn1, ffn1_gate, ffn1_up, ffn1_down, norm_mix, w_in, conv_w, hgrn_lb_logits,
           hgrn_norm, conv_norm, w_out, norm_ffn2, ffn2_gate, ffn2_up, ffn2_down, norm_ple, w_ple,
           w_ple_gate, norm_final):
    bsz, s, d = x.shape
    depth = w_in.shape[0]
    assert depth == 1, "single-layer block"
    t = bsz * s
    hw = hgrn_lb_logits.shape[1]
    heads = hw // HEAD_DIM
    cw = conv_w.shape[2]
    assert w_in.shape[2] == 4 * hw + 3 * cw

    row = lambda a: a.reshape(1, -1)
    i = 0
    lb_logits = hgrn_lb_logits.reshape(-1, heads, 1, HEAD_DIM)

    h1, u = _ffn(x.reshape(t, d), row(norm_ffn1[i]), ffn1_gate[i], ffn1_up[i], ffn1_down[i],
                 row(norm_mix[i]), tm=1024, tf=512)
    u3 = u.reshape(bsz, s, d)
    o_h = _hgrn(u3, w_in[i], lb_logits, row(hgrn_norm[i]), heads=heads, ts=512, chunk=128,
              piece_rows=128)
    y_c = _conv(u3, w_in[i], conv_w[i], row(conv_norm[i]), col0=4 * hw, ts=512, gw=256)
    h2 = _outproj(h1, o_h.reshape(t, hw), y_c.reshape(t, cw), w_out[i], tm=512)
    h3, n3 = _ffn(h2, row(norm_ffn2[i]), ffn2_gate[i], ffn2_up[i], ffn2_down[i],
                  row(norm_ple[i]), tm=1024, tf=512)
    y = _ple(h3, n3, p[i].reshape(t, -1), w_ple_gate[i], w_ple[i], row(norm_final), tm=512)
    return y.reshape(bsz, s, d)
```

```python
import functools

import jax
import jax.numpy as jnp
from jax import lax
from jax.experimental import pallas as pl
from jax.experimental.pallas import tpu as pltpu

EPS = 1e-6
HEAD_DIM = 128
SUBLANES = 8
MXU_COLS = 256
VMEM_LIMIT_BYTES = 56 * 1024 * 1024
FFN_VMEM_LIMIT_BYTES = 60 * 1024 * 1024

F32 = jnp.float32
BF16 = jnp.bfloat16


def _rms(x, gain):
    r = lax.rsqrt(jnp.mean(x * x, axis=-1, keepdims=True) + EPS)
    return (x * r) * gain


def _dot(a, b):
    return jnp.dot(a, b, preferred_element_type=F32)


def _dot_nt(a, b):
    return lax.dot_general(a, b, (((1,), (1,)), ((), ())), preferred_element_type=F32)


def _dot_tn(a, b):
    return lax.dot_general(a, b, (((0,), (0,)), ((), ())), preferred_element_type=F32)


def _resident(shape, index_map):
    return pl.BlockSpec(shape, index_map, pipeline_mode=pl.Buffered(1))


def _ffn_kernel(x_hbm, g_ref, wg_ref, wu_ref, wd_ref, gnext_ref, h_hbm, n_ref,
                acc_ref, xn_ref, in_sem, out_sem):
    i, f = pl.program_id(0), pl.program_id(1)
    nt, nf = pl.num_programs(0), pl.num_programs(1)
    tm = xn_ref.shape[0]
    slot = i % 2

    def x_copy(tile, s):
        return pltpu.make_async_copy(x_hbm.at[pl.ds(tile * tm, tm)], acc_ref.at[s], in_sem.at[s])

    def h_copy(tile, s):
        return pltpu.make_async_copy(acc_ref.at[s], h_hbm.at[pl.ds(tile * tm, tm)], out_sem.at[s])

    @pl.when(f == 0)
    def _():
        @pl.when(i == 0)
        def _():
            x_copy(0, 0).start()

        x_copy(i, slot).wait()
        xn_ref[...] = _rms(acc_ref[slot], g_ref[...]).astype(BF16)

    @pl.when(f == 1)
    def _():
        @pl.when(i > 0)
        def _():
            h_copy(i - 1, 1 - slot).wait()

        @pl.when(i + 1 < nt)
        def _():
            x_copy(i + 1, 1 - slot).start()

    xn = xn_ref[...]
    gate = _dot(xn, wg_ref[...].astype(BF16))
    up = _dot(xn, wu_ref[...].astype(BF16))
    act = ((0.5 * gate) * jax.nn.sigmoid(gate) * up).astype(BF16)
    acc_ref[slot] += _dot(act, wd_ref[...].astype(BF16))

    @pl.when(f == nf - 1)
    def _():
        n_ref[...] = _rms(acc_ref[slot], gnext_ref[...]).astype(BF16)
        h_copy(i, slot).start()

        @pl.when(i == nt - 1)
        def _():
            h_copy(i, slot).wait()


def _ffn(x, gain, wg, wu, wd, gain_next, *, tm, tf):
    t, d = x.shape
    dff = wg.shape[1]
    assert dff // tf >= 2, "the accumulator hand-over happens at the second F step"
    return pl.pallas_call(
        _ffn_kernel,
        grid=(t // tm, dff // tf),
        in_specs=[
            pl.BlockSpec(memory_space=pl.ANY),
            pl.BlockSpec((1, d), lambda i, f: (0, 0)),
            pl.BlockSpec((d, tf), lambda i, f: (0, f)),
            pl.BlockSpec((d, tf), lambda i, f: (0, f)),
            pl.BlockSpec((tf, d), lambda i, f: (f, 0)),
            pl.BlockSpec((1, d), lambda i, f: (0, 0)),
        ],
        out_specs=[
            pl.BlockSpec(memory_space=pl.ANY),
            pl.BlockSpec((tm, d), lambda i, f: (i, 0)),
        ],
        out_shape=[
            jax.ShapeDtypeStruct((t, d), F32),
            jax.ShapeDtypeStruct((t, d), BF16),
        ],
        scratch_shapes=[
            pltpu.VMEM((2, tm, d), F32),
            pltpu.VMEM((tm, d), BF16),
            pltpu.SemaphoreType.DMA((2,)),
            pltpu.SemaphoreType.DMA((2,)),
        ],
        compiler_params=pltpu.CompilerParams(
            dimension_semantics=("arbitrary", "arbitrary"),
            vmem_limit_bytes=FFN_VMEM_LIMIT_BYTES),
        name="ffn",
    )(x, gain, wg, wu, wd, gain_next)


def _bcast_row(x, blk, r):
    c, l = x.shape
    x3 = x.reshape(c // blk, blk, l)
    return jnp.broadcast_to(x3[:, r:r + 1, :], (c // blk, blk, l)).reshape(c, l)


def _chunk_cumsum(x, tri):
    c, w = tri.shape[0], x.shape[1]
    hi = x.astype(BF16)
    r1 = x - hi.astype(F32)
    mid = r1.astype(BF16)
    lo = (r1 - mid.astype(F32)).astype(BF16)
    parts = jnp.concatenate([hi, mid, lo], axis=1)
    cs = jnp.concatenate([_dot(tri, parts[c0:c0 + c]) for c0 in range(0, x.shape[0], c)], axis=0)
    return cs[:, 0:w] + cs[:, w:2 * w] + cs[:, 2 * w:3 * w]


def _split_distance(b2, l2, m, sub):
    c = b2.shape[0]
    if m == 2:
        return jnp.where(sub % 2 == 1, l2, 0.0)
    if m == 4:
        return b2 - jnp.where(sub < 4, _bcast_row(b2, SUBLANES, 1), _bcast_row(b2, SUBLANES, 5))
    if m == SUBLANES:
        return b2 - _bcast_row(b2, SUBLANES, 3)
    half = m // 2
    return jnp.concatenate(
        [b2[r0:r0 + m] - b2[r0 + half - 1:r0 + half] for r0 in range(0, c, m)], axis=0)


def _hgrn_tile(p_ref, lb, gn, state_t, o_ref, chunk, side_work):
    ts = p_ref.shape[0]
    starts = range(0, ts, chunk)
    rows = lambda x, c0: x[c0:c0 + chunk]
    q = p_ref[:, 0 * HEAD_DIM:1 * HEAD_DIM]
    fl = p_ref[:, 1 * HEAD_DIM:2 * HEAD_DIM]
    v = p_ref[:, 2 * HEAD_DIM:3 * HEAD_DIM].astype(BF16)
    g = p_ref[:, 3 * HEAD_DIM:4 * HEAD_DIM]

    tt = lax.broadcasted_iota(jnp.int32, (chunk, chunk), 0)
    ss = lax.broadcasted_iota(jnp.int32, (chunk, chunk), 1)
    tri = (ss <= tt).astype(BF16)
    diag = chunk.bit_length() - 1
    level = jnp.where(ss < tt, 31 - lax.clz(tt ^ ss), jnp.where(ss == tt, diag, -1))

    f = lb + (1.0 - lb) * jax.nn.sigmoid(fl)
    k = 1.0 - f
    qf = q * jax.nn.sigmoid(q)
    l2 = jnp.log2(f)
    b2 = _chunk_cumsum(l2, tri)

    for work in side_work:
        work()

    qs = (qf * jnp.exp2(b2)).astype(BF16)
    to_end = jnp.concatenate([b2[c0 + chunk - 1:c0 + chunk] - rows(b2, c0) for c0 in starts], axis=0)
    kd = (k * jnp.exp2(to_end)).astype(BF16)
    carried = []
    for c0 in starts:
        carried.append(_dot_nt(rows(qs, c0), state_t.astype(BF16)))
        state_t = state_t * jnp.exp2(b2[c0 + chunk - 1:c0 + chunk]) + _dot_tn(rows(v, c0), rows(kd, c0))

    sub = lax.broadcasted_iota(jnp.int32, (ts, HEAD_DIM), 0) % SUBLANES
    operands = [(diag, qf.astype(BF16), k.astype(BF16))]
    m, li = 2, 0
    while m <= chunk:
        e = jnp.exp2(-jnp.abs(_split_distance(b2, l2, m, sub)))
        operands.append((li, (qf * e).astype(BF16), (k * e).astype(BF16)))
        m, li = 2 * m, li + 1
    scores = [jnp.zeros((chunk, chunk), F32) for _ in starts]
    for li, qe, ke in operands:
        scores = [jnp.where(level == li, _dot_nt(rows(qe, c0), rows(ke, c0)), sc)
                  for c0, sc in zip(starts, scores)]

    o = jnp.concatenate([_dot(sc.astype(BF16), rows(v, c0)) + oc for c0, sc, oc in zip(starts, scores, carried)],
                        axis=0)
    o_ref[0] = (_rms(o, gn) * (g * jax.nn.sigmoid(g))).astype(o_ref.dtype)
    return state_t


def _hgrn_kernel(u_ref, wq_ref, wf_ref, wi_ref, wg_ref, lbl_ref, gn_ref, o_ref,
                 w_ref, pa_ref, pb_ref, state_ref, *, chunk, piece_rows, ni, n):
    j = pl.program_id(0)
    ts = pa_ref.shape[0]

    @pl.when(j == 0)
    def _():
        pb_ref[...] = jnp.zeros_like(pb_ref)
        state_ref[...] = jnp.zeros_like(state_ref)

    @pl.when((j % ni == 0) & (j < n))
    def _():
        for sec, src in enumerate((wq_ref, wf_ref, wi_ref, wg_ref)):
            w_ref[:, sec * HEAD_DIM:(sec + 1) * HEAD_DIM] = src[...].astype(BF16)

    def step(p_new_ref, p_ref):
        def piece(r0, c0):
            p_new_ref[r0:r0 + piece_rows, c0:c0 + MXU_COLS] = _dot(
                u_ref[0, r0:r0 + piece_rows, :], w_ref[:, c0:c0 + MXU_COLS])

        pieces = [functools.partial(piece, r0, c0)
                  for r0 in range(0, ts, piece_rows) for c0 in range(0, 4 * HEAD_DIM, MXU_COLS)]

        logits = [lbl_ref[l, 0] for l in range(lbl_ref.shape[0])]
        mx = functools.reduce(jnp.maximum, logits)
        ex = [jnp.exp(l - mx) for l in logits]
        lb = ex[0] / functools.reduce(jnp.add, ex)

        first = (jnp.maximum(j - 1, 0) % ni) == 0
        state_t = jnp.where(first, 0.0, state_ref[...])
        state_ref[...] = _hgrn_tile(p_ref, lb, gn_ref[...], state_t, o_ref, chunk, pieces)

    @pl.when(j % 2 == 0)
    def _():
        step(pa_ref, pb_ref)

    @pl.when(j % 2 == 1)
    def _():
        step(pb_ref, pa_ref)


def _hgrn(u, w_in, lb_logits, gn, *, heads, ts, chunk, piece_rows):
    bsz, s, d = u.shape
    nl = lb_logits.shape[0]
    ni = s // ts
    n = bsz * heads * ni

    def cur(j):
        jc = jnp.minimum(j, n - 1)
        return jc // (heads * ni), (jc // ni) % heads, jc % ni

    def prev(j):
        return cur(jnp.maximum(j - 1, 0))

    w_spec = lambda sec: pl.BlockSpec((d, HEAD_DIM), lambda j: (0, sec * heads + cur(j)[1]))
    return pl.pallas_call(
        functools.partial(_hgrn_kernel, chunk=chunk, piece_rows=piece_rows, ni=ni, n=n),
        grid=(n + 1,),
        in_specs=[
            pl.BlockSpec((1, ts, d), lambda j: (cur(j)[0], cur(j)[2], 0)),
            w_spec(0), w_spec(1), w_spec(2), w_spec(3),
            pl.BlockSpec((nl, 1, 1, HEAD_DIM), lambda j: (0, prev(j)[1], 0, 0)),
            pl.BlockSpec((1, HEAD_DIM), lambda j: (0, 0)),
        ],
        out_specs=pl.BlockSpec((1, ts, HEAD_DIM), lambda j: (prev(j)[0], prev(j)[2], prev(j)[1])),
        out_shape=jax.ShapeDtypeStruct((bsz, s, heads * HEAD_DIM), BF16),
        scratch_shapes=[
            pltpu.VMEM((d, 4 * HEAD_DIM), BF16),
            pltpu.VMEM((ts, 4 * HEAD_DIM), F32),
            pltpu.VMEM((ts, 4 * HEAD_DIM), F32),
            pltpu.VMEM((HEAD_DIM, HEAD_DIM), F32),
        ],
        compiler_params=pltpu.CompilerParams(
            dimension_semantics=("arbitrary",),
            vmem_limit_bytes=VMEM_LIMIT_BYTES),
        name="hgrn",
    )(u, w_in, w_in, w_in, w_in, lb_logits, gn)


def _conv_kernel(u_ref, wb_ref, wc_ref, wv_ref, cw_ref, gn_ref, o_ref,
                 w_ref, pa_ref, pb_ref, halo_ref, *, ni, sweep, n):
    j = pl.program_id(0)
    ts, gw = o_ref.shape[1], o_ref.shape[2]

    @pl.when(j == 0)
    def _():
        pb_ref[...] = jnp.zeros_like(pb_ref)
        halo_ref[...] = jnp.zeros_like(halo_ref)

    @pl.when((j % sweep == 0) & (j < n))
    def _():
        for sec, src in enumerate((wb_ref, wc_ref, wv_ref)):
            w_ref[:, sec * gw:(sec + 1) * gw] = src[...].astype(BF16)

    def step(p_new_ref, p_ref):
        p_new_ref[...] = _dot(u_ref[0], w_ref[...])

        first = (jnp.maximum(j - 1, 0) % ni) == 0
        bg = p_ref[:, 0:gw]
        uc = p_ref[:, gw:2 * gw] * p_ref[:, 2 * gw:3 * gw]
        row = lax.broadcasted_iota(jnp.int32, (ts, gw), 0)
        prev1 = jnp.where(first, 0.0, halo_ref[1:2, :])
        prev2 = jnp.where(first, 0.0, halo_ref[0:1, :])
        u1 = jnp.where(row >= 1, pltpu.roll(uc, 1, axis=0), prev1)
        u2 = jnp.where(row >= 2, pltpu.roll(uc, 2, axis=0), jnp.where(row == 1, prev1, prev2))
        halo_ref[...] = uc[ts - 2:ts, :]
        y = bg * (cw_ref[0:1, :] * u2 + cw_ref[1:2, :] * u1 + cw_ref[2:3, :] * uc)
        for g0 in range(0, gw, HEAD_DIM):
            yg = y[:, g0:g0 + HEAD_DIM]
            o_ref[0, :, g0:g0 + HEAD_DIM] = _rms(yg, gn_ref[:, g0:g0 + HEAD_DIM]).astype(o_ref.dtype)

    @pl.when(j % 2 == 0)
    def _():
        step(pa_ref, pb_ref)

    @pl.when(j % 2 == 1)
    def _():
        step(pb_ref, pa_ref)


def _conv(u, w_in, conv_w, gn, *, col0, ts, gw):
    bsz, s, d = u.shape
    cw = conv_w.shape[1]
    ni = s // ts
    sweep = bsz * ni
    n = (cw // gw) * sweep

    def cur(j):
        jc = jnp.minimum(j, n - 1)
        return jc // sweep, (jc // ni) % bsz, jc % ni

    def prev(j):
        return cur(jnp.maximum(j - 1, 0))

    w_spec = lambda sec: pl.BlockSpec((d, gw), lambda j: (0, (col0 + sec * cw) // gw + cur(j)[0]))
    return pl.pallas_call(
        functools.partial(_conv_kernel, ni=ni, sweep=sweep, n=n),
        grid=(n + 1,),
        in_specs=[
            pl.BlockSpec((1, ts, d), lambda j: (cur(j)[1], cur(j)[2], 0)),
            w_spec(0), w_spec(1), w_spec(2),
            pl.BlockSpec((3, gw), lambda j: (0, prev(j)[0])),
            pl.BlockSpec((1, gw), lambda j: (0, prev(j)[0])),
        ],
        out_specs=pl.BlockSpec((1, ts, gw), lambda j: (prev(j)[1], prev(j)[2], prev(j)[0])),
        out_shape=jax.ShapeDtypeStruct((bsz, s, cw), BF16),
        scratch_shapes=[
            pltpu.VMEM((d, 3 * gw), BF16),
            pltpu.VMEM((ts, 3 * gw), F32),
            pltpu.VMEM((ts, 3 * gw), F32),
            pltpu.VMEM((2, gw), F32),
        ],
        compiler_params=pltpu.CompilerParams(
            dimension_semantics=("arbitrary",),
            vmem_limit_bytes=VMEM_LIMIT_BYTES),
        name="conv",
    )(u, w_in, w_in, w_in, conv_w, gn)


def _outproj_kernel(h_ref, a_ref, b_ref, wo_ref, o_ref, w_ref):
    @pl.when(pl.program_id(0) == 0)
    def _():
        w_ref[...] = wo_ref[...].astype(BF16)

    ka = a_ref.shape[1]
    o_ref[...] = h_ref[...] + _dot(a_ref[...], w_ref[0:ka, :]) + _dot(b_ref[...], w_ref[ka:, :])


def _outproj(h, a, b, wo, *, tm):
    t, d = h.shape
    ka, kb = a.shape[1], b.shape[1]
    return pl.pallas_call(
        _outproj_kernel,
        grid=(t // tm,),
        in_specs=[
            pl.BlockSpec((tm, d), lambda i: (i, 0)),
            pl.BlockSpec((tm, ka), lambda i: (i, 0)),
            pl.BlockSpec((tm, kb), lambda i: (i, 0)),
            _resident((ka + kb, d), lambda i: (0, 0)),
        ],
        out_specs=pl.BlockSpec((tm, d), lambda i: (i, 0)),
        out_shape=jax.ShapeDtypeStruct((t, d), F32),
        scratch_shapes=[pltpu.VMEM((ka + kb, d), BF16)],
        compiler_params=pltpu.CompilerParams(
            dimension_semantics=("arbitrary",),
            vmem_limit_bytes=VMEM_LIMIT_BYTES),
        name="outproj",
    )(h, a, b, wo)


def _ple_kernel(h_ref, n_ref, p_ref, wg_ref, we_ref, gf_ref, o_ref, w_ref):
    @pl.when(pl.program_id(0) == 0)
    def _():
        w_ref[...] = wg_ref[...].astype(BF16)

    gate = jax.nn.sigmoid(_dot(n_ref[...], w_ref[...]))
    pe = _dot(p_ref[...].astype(BF16), we_ref[...].astype(BF16))
    o_ref[...] = _rms(h_ref[...] + gate * pe, gf_ref[...])


def _ple(h, n, p, wg, we, gf, *, tm):
    t, d = h.shape
    pd = p.shape[1]
    return pl.pallas_call(
        _ple_kernel,
        grid=(t // tm,),
        in_specs=[
            pl.BlockSpec((tm, d), lambda i: (i, 0)),
            pl.BlockSpec((tm, d), lambda i: (i, 0)),
            pl.BlockSpec((tm, pd), lambda i: (i, 0)),
            _resident((d, d), lambda i: (0, 0)),
            _resident((pd, d), lambda i: (0, 0)),
            pl.BlockSpec((1, d), lambda i: (0, 0)),
        ],
        out_specs=pl.BlockSpec((tm, d), lambda i: (i, 0)),
        out_shape=jax.ShapeDtypeStruct((t, d), F32),
        scratch_shapes=[pltpu.VMEM((d, d), BF16)],
        compiler_params=pltpu.CompilerParams(
            dimension_semantics=("arbitrary",),
            vmem_limit_bytes=VMEM_LIMIT_BYTES),
        name="ple",
    )(h, n, p, wg, we, gf)


def kernel(x, p, norm_ffn1, ffn1_gate, ffn1_up, ffn1_down, norm_mix, w_in, conv_w, hgrn_lb_logits,
           hgrn_norm, conv_norm, w_out, norm_ffn2, ffn2_gate, ffn2_up, ffn2_down, norm_ple, w_ple,
           w_ple_gate, norm_final):
    bsz, s, d = x.shape
    depth = w_in.shape[0]
    assert depth == 1, "single-layer block"
    t = bsz * s
    hw = hgrn_lb_logits.shape[1]
    heads = hw // HEAD_DIM
    cw = conv_w.shape[2]
    assert w_in.shape[2] == 4 * hw + 3 * cw

    row = lambda a: a.reshape(1, -1)
    i = 0
    lb_logits = hgrn_lb_logits.reshape(-1, heads, 1, HEAD_DIM)

    h1, u = _ffn(x.reshape(t, d), row(norm_ffn1[i]), ffn1_gate[i], ffn1_up[i], ffn1_down[i],
                 row(norm_mix[i]), tm=1024, tf=512)
    u3 = u.reshape(bsz, s, d)
    o_h = _hgrn(u3, w_in[i], lb_logits, row(hgrn_norm[i]), heads=heads, ts=512, chunk=128,
              piece_rows=128)
    y_c = _conv(u3, w_in[i], conv_w[i], row(conv_norm[i]), col0=4 * hw, ts=1024, gw=256)
    h2 = _outproj(h1, o_h.reshape(t, hw), y_c.reshape(t, cw), w_out[i], tm=512)
    h3, n3 = _ffn(h2, row(norm_ffn2[i]), ffn2_gate[i], ffn2_up[i], ffn2_down[i],
                  row(norm_ple[i]), tm=1024, tf=512)
    y = _ple(h3, n3, p[i].reshape(t, -1), w_ple_gate[i], w_ple[i], row(norm_final), tm=512)
    return y.reshape(bsz, s, d)
```

```python
import functools

import jax
import jax.numpy as jnp
from jax import lax
from jax.experimental import pallas as pl
from jax.experimental.pallas import tpu as pltpu

EPS = 1e-6
HEAD_DIM = 128
SUBLANES = 8
MXU_COLS = 256
VMEM_LIMIT_BYTES = 56 * 1024 * 1024
FFN_VMEM_LIMIT_BYTES = 60 * 1024 * 1024

F32 = jnp.float32
BF16 = jnp.bfloat16


def _rms(x, gain):
    r = lax.rsqrt(jnp.mean(x * x, axis=-1, keepdims=True) + EPS)
    return (x * r) * gain


def _dot(a, b):
    return jnp.dot(a, b, preferred_element_type=F32)


def _dot_nt(a, b):
    return lax.dot_general(a, b, (((1,), (1,)), ((), ())), preferred_element_type=F32)


def _dot_tn(a, b):
    return lax.dot_general(a, b, (((0,), (0,)), ((), ())), preferred_element_type=F32)


def _resident(shape, index_map):
    return pl.BlockSpec(shape, index_map, pipeline_mode=pl.Buffered(1))


def _ffn_kernel(x_hbm, g_ref, wg_ref, wu_ref, wd_ref, *rest):
    if len(rest) == 7:
        gnext_ref, h_hbm, n_ref, acc_ref, xn_ref, in_sem, out_sem = rest
    else:
        (h_hbm, acc_ref, xn_ref, in_sem, out_sem), gnext_ref, n_ref = rest, None, None
    i, f = pl.program_id(0), pl.program_id(1)
    nt, nf = pl.num_programs(0), pl.num_programs(1)
    tm = xn_ref.shape[0]
    slot = i % 2

    def x_copy(tile, s):
        return pltpu.make_async_copy(x_hbm.at[pl.ds(tile * tm, tm)], acc_ref.at[s], in_sem.at[s])

    def h_copy(tile, s):
        return pltpu.make_async_copy(acc_ref.at[s], h_hbm.at[pl.ds(tile * tm, tm)], out_sem.at[s])

    @pl.when(f == 0)
    def _():
        @pl.when(i == 0)
        def _():
            x_copy(0, 0).start()

        x_copy(i, slot).wait()
        xn_ref[...] = _rms(acc_ref[slot], g_ref[...]).astype(BF16)

    @pl.when(f == 1)
    def _():
        @pl.when(i > 0)
        def _():
            h_copy(i - 1, 1 - slot).wait()

        @pl.when(i + 1 < nt)
        def _():
            x_copy(i + 1, 1 - slot).start()

    xn = xn_ref[...]
    gate = _dot(xn, wg_ref[...].astype(BF16))
    up = _dot(xn, wu_ref[...].astype(BF16))
    act = ((0.5 * gate) * jax.nn.sigmoid(gate) * up).astype(BF16)
    acc_ref[slot] += _dot(act, wd_ref[...].astype(BF16))

    @pl.when(f == nf - 1)
    def _():
        if n_ref is not None:
            n_ref[...] = _rms(acc_ref[slot], gnext_ref[...]).astype(BF16)
        h_copy(i, slot).start()

        @pl.when(i == nt - 1)
        def _():
            h_copy(i, slot).wait()


def _ffn(x, gain, wg, wu, wd, gain_next, *, tm, tf):
    t, d = x.shape
    dff = wg.shape[1]
    assert dff // tf >= 2, "the accumulator hand-over happens at the second F step"
    emit = gain_next is not None
    gain_spec = pl.BlockSpec((1, d), lambda i, f: (0, 0))
    return pl.pallas_call(
        _ffn_kernel,
        grid=(t // tm, dff // tf),
        in_specs=[
            pl.BlockSpec(memory_space=pl.ANY),
            gain_spec,
            pl.BlockSpec((d, tf), lambda i, f: (0, f)),
            pl.BlockSpec((d, tf), lambda i, f: (0, f)),
            pl.BlockSpec((tf, d), lambda i, f: (f, 0)),
        ] + [gain_spec] * emit,
        out_specs=[pl.BlockSpec(memory_space=pl.ANY)] + [pl.BlockSpec((tm, d), lambda i, f: (i, 0))] * emit,
        out_shape=[jax.ShapeDtypeStruct((t, d), F32)] + [jax.ShapeDtypeStruct((t, d), BF16)] * emit,
        scratch_shapes=[
            pltpu.VMEM((2, tm, d), F32),
            pltpu.VMEM((tm, d), BF16),
            pltpu.SemaphoreType.DMA((2,)),
            pltpu.SemaphoreType.DMA((2,)),
        ],
        compiler_params=pltpu.CompilerParams(
            dimension_semantics=("arbitrary", "arbitrary"),
            vmem_limit_bytes=FFN_VMEM_LIMIT_BYTES),
        name="ffn",
    )(x, gain, wg, wu, wd, *([gain_next] * emit))


def _bcast_row(x, blk, r):
    c, l = x.shape
    x3 = x.reshape(c // blk, blk, l)
    return jnp.broadcast_to(x3[:, r:r + 1, :], (c // blk, blk, l)).reshape(c, l)


def _chunk_cumsum(x, tri):
    c, w = tri.shape[0], x.shape[1]
    hi = x.astype(BF16)
    r1 = x - hi.astype(F32)
    mid = r1.astype(BF16)
    lo = (r1 - mid.astype(F32)).astype(BF16)
    parts = jnp.concatenate([hi, mid, lo], axis=1)
    cs = jnp.concatenate([_dot(tri, parts[c0:c0 + c]) for c0 in range(0, x.shape[0], c)], axis=0)
    return cs[:, 0:w] + cs[:, w:2 * w] + cs[:, 2 * w:3 * w]


def _split_distance(b2, l2, m, sub):
    c = b2.shape[0]
    if m == 2:
        return jnp.where(sub % 2 == 1, l2, 0.0)
    if m == 4:
        return b2 - jnp.where(sub < 4, _bcast_row(b2, SUBLANES, 1), _bcast_row(b2, SUBLANES, 5))
    if m == SUBLANES:
        return b2 - _bcast_row(b2, SUBLANES, 3)
    half = m // 2
    return jnp.concatenate(
        [b2[r0:r0 + m] - b2[r0 + half - 1:r0 + half] for r0 in range(0, c, m)], axis=0)


def _hgrn_tile(p_ref, lb, gn, state_t, o_ref, chunk, side_work, side_slots):
    ts = p_ref.shape[0]
    starts = range(0, ts, chunk)
    rows = lambda x, c0: x[c0:c0 + chunk]
    q = p_ref[:, 0 * HEAD_DIM:1 * HEAD_DIM]
    fl = p_ref[:, 1 * HEAD_DIM:2 * HEAD_DIM]
    v = p_ref[:, 2 * HEAD_DIM:3 * HEAD_DIM].astype(BF16)
    g = p_ref[:, 3 * HEAD_DIM:4 * HEAD_DIM]

    tt = lax.broadcasted_iota(jnp.int32, (chunk, chunk), 0)
    ss = lax.broadcasted_iota(jnp.int32, (chunk, chunk), 1)
    tri = (ss <= tt).astype(BF16)
    diag = chunk.bit_length() - 1
    level = jnp.where(ss < tt, 31 - lax.clz(tt ^ ss), jnp.where(ss == tt, diag, -1))

    f = lb + (1.0 - lb) * jax.nn.sigmoid(fl)
    k = 1.0 - f
    qf = q * jax.nn.sigmoid(q)
    l2 = jnp.log2(f)
    def side(slot):
        for work, s in zip(side_work, side_slots):
            if s == slot:
                work()

    side(0)
    b2 = _chunk_cumsum(l2, tri)

    qs = (qf * jnp.exp2(b2)).astype(BF16)
    to_end = jnp.concatenate([b2[c0 + chunk - 1:c0 + chunk] - rows(b2, c0) for c0 in starts], axis=0)
    kd = (k * jnp.exp2(to_end)).astype(BF16)
    carried = []
    for c0 in starts:
        carried.append(_dot_nt(rows(qs, c0), state_t.astype(BF16)))
        state_t = state_t * jnp.exp2(b2[c0 + chunk - 1:c0 + chunk]) + _dot_tn(rows(v, c0), rows(kd, c0))

    sub = lax.broadcasted_iota(jnp.int32, (ts, HEAD_DIM), 0) % SUBLANES
    qb, kb = qf.astype(BF16), k.astype(BF16)
    side(1)
    scores = [jnp.where(level == diag, _dot_nt(rows(qb, c0), rows(kb, c0)), 0.0) for c0 in starts]
    m, li = 2, 0
    while m <= chunk:
        e = jnp.exp2(-jnp.abs(_split_distance(b2, l2, m, sub)))
        qe, ke = (qf * e).astype(BF16), (k * e).astype(BF16)
        side(2 + li)
        scores = [jnp.where(level == li, _dot_nt(rows(qe, c0), rows(ke, c0)), sc)
                  for c0, sc in zip(starts, scores)]
        m, li = 2 * m, li + 1

    o = [_dot(sc.astype(BF16), rows(v, c0)) + oc for c0, sc, oc in zip(starts, scores, carried)]
    side(2 + diag)
    o = jnp.concatenate(o, axis=0)
    o_ref[0] = (_rms(o, gn) * (g * jax.nn.sigmoid(g))).astype(o_ref.dtype)
    return state_t


def _hgrn_kernel(u_ref, wq_ref, wf_ref, wi_ref, wg_ref, lbl_ref, gn_ref, o_ref,
                 w_ref, pa_ref, pb_ref, state_ref, *, chunk, piece_rows, piece_slots, ni, n):
    j = pl.program_id(0)
    ts = pa_ref.shape[0]

    @pl.when(j == 0)
    def _():
        pb_ref[...] = jnp.zeros_like(pb_ref)
        state_ref[...] = jnp.zeros_like(state_ref)

    @pl.when((j % ni == 0) & (j < n))
    def _():
        for sec, src in enumerate((wq_ref, wf_ref, wi_ref, wg_ref)):
            w_ref[:, sec * HEAD_DIM:(sec + 1) * HEAD_DIM] = src[...].astype(BF16)

    def step(p_new_ref, p_ref):
        def piece(r0, c0):
            p_new_ref[r0:r0 + piece_rows, c0:c0 + MXU_COLS] = _dot(
                u_ref[0, r0:r0 + piece_rows, :], w_ref[:, c0:c0 + MXU_COLS])

        pieces = [functools.partial(piece, r0, c0)
                  for r0 in range(0, ts, piece_rows) for c0 in range(0, 4 * HEAD_DIM, MXU_COLS)]

        logits = [lbl_ref[l, 0] for l in range(lbl_ref.shape[0])]
        mx = functools.reduce(jnp.maximum, logits)
        ex = [jnp.exp(l - mx) for l in logits]
        lb = ex[0] / functools.reduce(jnp.add, ex)

        first = (jnp.maximum(j - 1, 0) % ni) == 0
        state_t = jnp.where(first, 0.0, state_ref[...])
        state_ref[...] = _hgrn_tile(p_ref, lb, gn_ref[...], state_t, o_ref, chunk, pieces, piece_slots)

    @pl.when(j % 2 == 0)
    def _():
        step(pa_ref, pb_ref)

    @pl.when(j % 2 == 1)
    def _():
        step(pb_ref, pa_ref)


def _hgrn(u, w_in, lb_logits, gn, *, heads, ts, chunk, piece_rows, piece_slots):
    bsz, s, d = u.shape
    nl = lb_logits.shape[0]
    ni = s // ts
    n = bsz * heads * ni

    def cur(j):
        jc = jnp.minimum(j, n - 1)
        return jc // (heads * ni), (jc // ni) % heads, jc % ni

    def prev(j):
        return cur(jnp.maximum(j - 1, 0))

    w_spec = lambda sec: pl.BlockSpec((d, HEAD_DIM), lambda j: (0, sec * heads + cur(j)[1]))
    return pl.pallas_call(
        functools.partial(_hgrn_kernel, chunk=chunk, piece_rows=piece_rows, piece_slots=piece_slots, ni=ni, n=n),
        grid=(n + 1,),
        in_specs=[
            pl.BlockSpec((1, ts, d), lambda j: (cur(j)[0], cur(j)[2], 0)),
            w_spec(0), w_spec(1), w_spec(2), w_spec(3),
            pl.BlockSpec((nl, 1, 1, HEAD_DIM), lambda j: (0, prev(j)[1], 0, 0)),
            pl.BlockSpec((1, HEAD_DIM), lambda j: (0, 0)),
        ],
        out_specs=pl.BlockSpec((1, ts, HEAD_DIM), lambda j: (prev(j)[0], prev(j)[2], prev(j)[1])),
        out_shape=jax.ShapeDtypeStruct((bsz, s, heads * HEAD_DIM), BF16),
        scratch_shapes=[
            pltpu.VMEM((d, 4 * HEAD_DIM), BF16),
            pltpu.VMEM((ts, 4 * HEAD_DIM), F32),
            pltpu.VMEM((ts, 4 * HEAD_DIM), F32),
            pltpu.VMEM((HEAD_DIM, HEAD_DIM), F32),
        ],
        compiler_params=pltpu.CompilerParams(
            dimension_semantics=("arbitrary",),
            vmem_limit_bytes=VMEM_LIMIT_BYTES),
        name="hgrn",
    )(u, w_in, w_in, w_in, w_in, lb_logits, gn)


def _conv_kernel(u_ref, wb_ref, wc_ref, wv_ref, cw_ref, gn_ref, o_ref,
                 w_ref, pa_ref, pb_ref, halo_ref, *, ni, sweep, n):
    j = pl.program_id(0)
    ts, gw = o_ref.shape[1], o_ref.shape[2]

    @pl.when(j == 0)
    def _():
        pb_ref[...] = jnp.zeros_like(pb_ref)
        halo_ref[...] = jnp.zeros_like(halo_ref)

    @pl.when((j % sweep == 0) & (j < n))
    def _():
        for sec, src in enumerate((wb_ref, wc_ref, wv_ref)):
            w_ref[:, sec * gw:(sec + 1) * gw] = src[...].astype(BF16)

    def step(p_new_ref, p_ref):
        p_new_ref[...] = _dot(u_ref[0], w_ref[...])

        first = (jnp.maximum(j - 1, 0) % ni) == 0
        bg = p_ref[:, 0:gw]
        uc = p_ref[:, gw:2 * gw] * p_ref[:, 2 * gw:3 * gw]
        row = lax.broadcasted_iota(jnp.int32, (ts, gw), 0)
        prev1 = jnp.where(first, 0.0, halo_ref[1:2, :])
        prev2 = jnp.where(first, 0.0, halo_ref[0:1, :])
        u1 = jnp.where(row >= 1, pltpu.roll(uc, 1, axis=0), prev1)
        u2 = jnp.where(row >= 2, pltpu.roll(uc, 2, axis=0), jnp.where(row == 1, prev1, prev2))
        halo_ref[...] = uc[ts - 2:ts, :]
        y = bg * (cw_ref[0:1, :] * u2 + cw_ref[1:2, :] * u1 + cw_ref[2:3, :] * uc)
        for g0 in range(0, gw, HEAD_DIM):
            yg = y[:, g0:g0 + HEAD_DIM]
            o_ref[0, :, g0:g0 + HEAD_DIM] = _rms(yg, gn_ref[:, g0:g0 + HEAD_DIM]).astype(o_ref.dtype)

    @pl.when(j % 2 == 0)
    def _():
        step(pa_ref, pb_ref)

    @pl.when(j % 2 == 1)
    def _():
        step(pb_ref, pa_ref)


def _conv(u, w_in, conv_w, gn, *, col0, ts, gw):
    bsz, s, d = u.shape
    cw = conv_w.shape[1]
    ni = s // ts
    sweep = bsz * ni
    n = (cw // gw) * sweep

    def cur(j):
        jc = jnp.minimum(j, n - 1)
        return jc // sweep, (jc // ni) % bsz, jc % ni

    def prev(j):
        return cur(jnp.maximum(j - 1, 0))

    w_spec = lambda sec: pl.BlockSpec((d, gw), lambda j: (0, (col0 + sec * cw) // gw + cur(j)[0]))
    return pl.pallas_call(
        functools.partial(_conv_kernel, ni=ni, sweep=sweep, n=n),
        grid=(n + 1,),
        in_specs=[
            pl.BlockSpec((1, ts, d), lambda j: (cur(j)[1], cur(j)[2], 0)),
            w_spec(0), w_spec(1), w_spec(2),
            pl.BlockSpec((3, gw), lambda j: (0, prev(j)[0])),
            pl.BlockSpec((1, gw), lambda j: (0, prev(j)[0])),
        ],
        out_specs=pl.BlockSpec((1, ts, gw), lambda j: (prev(j)[1], prev(j)[2], prev(j)[0])),
        out_shape=jax.ShapeDtypeStruct((bsz, s, cw), BF16),
        scratch_shapes=[
            pltpu.VMEM((d, 3 * gw), BF16),
            pltpu.VMEM((ts, 3 * gw), F32),
            pltpu.VMEM((ts, 3 * gw), F32),
            pltpu.VMEM((2, gw), F32),
        ],
        compiler_params=pltpu.CompilerParams(
            dimension_semantics=("arbitrary",),
            vmem_limit_bytes=VMEM_LIMIT_BYTES),
        name="conv",
    )(u, w_in, w_in, w_in, conv_w, gn)


def _outproj_kernel(h_ref, a_ref, b_ref, wo_ref, o_ref, w_ref):
    @pl.when(pl.program_id(0) == 0)
    def _():
        w_ref[...] = wo_ref[...].astype(BF16)

    ka = a_ref.shape[1]
    o_ref[...] = h_ref[...] + _dot(a_ref[...], w_ref[0:ka, :]) + _dot(b_ref[...], w_ref[ka:, :])


def _outproj(h, a, b, wo, *, tm):
    t, d = h.shape
    ka, kb = a.shape[1], b.shape[1]
    return pl.pallas_call(
        _outproj_kernel,
        grid=(t // tm,),
        in_specs=[
            pl.BlockSpec((tm, d), lambda i: (i, 0)),
            pl.BlockSpec((tm, ka), lambda i: (i, 0)),
            pl.BlockSpec((tm, kb), lambda i: (i, 0)),
            _resident((ka + kb, d), lambda i: (0, 0)),
        ],
        out_specs=pl.BlockSpec((tm, d), lambda i: (i, 0)),
        out_shape=jax.ShapeDtypeStruct((t, d), F32),
        scratch_shapes=[pltpu.VMEM((ka + kb, d), BF16)],
        compiler_params=pltpu.CompilerParams(
            dimension_semantics=("arbitrary",),
            vmem_limit_bytes=VMEM_LIMIT_BYTES),
        name="outproj",
    )(h, a, b, wo)


def _ple_kernel(h_ref, gn_ref, p_ref, wg_ref, we_ref, gf_ref, o_ref, w_ref):
    @pl.when(pl.program_id(0) == 0)
    def _():
        w_ref[...] = wg_ref[...].astype(BF16)

    h = h_ref[...]
    gate = jax.nn.sigmoid(_dot(_rms(h, gn_ref[...]).astype(BF16), w_ref[...]))
    pe = _dot(p_ref[...].astype(BF16), we_ref[...].astype(BF16))
    o_ref[...] = _rms(h + gate * pe, gf_ref[...])


def _ple(h, gn, p, wg, we, gf, *, tm):
    t, d = h.shape
    pd = p.shape[1]
    return pl.pallas_call(
        _ple_kernel,
        grid=(t // tm,),
        in_specs=[
            pl.BlockSpec((tm, d), lambda i: (i, 0)),
            pl.BlockSpec((1, d), lambda i: (0, 0)),
            pl.BlockSpec((tm, pd), lambda i: (i, 0)),
            _resident((d, d), lambda i: (0, 0)),
            _resident((pd, d), lambda i: (0, 0)),
            pl.BlockSpec((1, d), lambda i: (0, 0)),
        ],
        out_specs=pl.BlockSpec((tm, d), lambda i: (i, 0)),
        out_shape=jax.ShapeDtypeStruct((t, d), F32),
        scratch_shapes=[pltpu.VMEM((d, d), BF16)],
        compiler_params=pltpu.CompilerParams(
            dimension_semantics=("arbitrary",),
            vmem_limit_bytes=VMEM_LIMIT_BYTES),
        name="ple",
    )(h, gn, p, wg, we, gf)


def kernel(x, p, norm_ffn1, ffn1_gate, ffn1_up, ffn1_down, norm_mix, w_in, conv_w, hgrn_lb_logits,
           hgrn_norm, conv_norm, w_out, norm_ffn2, ffn2_gate, ffn2_up, ffn2_down, norm_ple, w_ple,
           w_ple_gate, norm_final):
    bsz, s, d = x.shape
    depth = w_in.shape[0]
    assert depth == 1, "single-layer block"
    t = bsz * s
    hw = hgrn_lb_logits.shape[1]
    heads = hw // HEAD_DIM
    cw = conv_w.shape[2]
    assert w_in.shape[2] == 4 * hw + 3 * cw

    row = lambda a: a.reshape(1, -1)
    i = 0
    lb_logits = hgrn_lb_logits.reshape(-1, heads, 1, HEAD_DIM)

    h1, u = _ffn(x.reshape(t, d), row(norm_ffn1[i]), ffn1_gate[i], ffn1_up[i], ffn1_down[i],
                 row(norm_mix[i]), tm=1024, tf=512)
    u3 = u.reshape(bsz, s, d)
    o_h = _hgrn(u3, w_in[i], lb_logits, row(hgrn_norm[i]), heads=heads, ts=512, chunk=128,
              piece_rows=128, piece_slots=(0, 0, 0, 2, 4, 6, 9, 9))
    y_c = _conv(u3, w_in[i], conv_w[i], row(conv_norm[i]), col0=4 * hw, ts=1024, gw=256)
    h2 = _outproj(h1, o_h.reshape(t, hw), y_c.reshape(t, cw), w_out[i], tm=512)
    (h3,) = _ffn(h2, row(norm_ffn2[i]), ffn2_gate[i], ffn2_up[i], ffn2_down[i], None, tm=1024, tf=512)
    y = _ple(h3, row(norm_ple[i]), p[i].reshape(t, -1), w_ple_gate[i], w_ple[i], row(norm_final), tm=512)
    return y.reshape(bsz, s, d)
```

```python
import functools

import jax
import jax.numpy as jnp
from jax import lax
from jax.experimental import pallas as pl
from jax.experimental.pallas import tpu as pltpu

EPS = 1e-6
HEAD_DIM = 128
SUBLANES = 8
MXU_COLS = 256
VMEM_LIMIT_BYTES = 56 * 1024 * 1024
FFN_VMEM_LIMIT_BYTES = 60 * 1024 * 1024

F32 = jnp.float32
BF16 = jnp.bfloat16


def _rms(x, gain):
    r = lax.rsqrt(jnp.mean(x * x, axis=-1, keepdims=True) + EPS)
    return (x * r) * gain


def _dot(a, b):
    return jnp.dot(a, b, preferred_element_type=F32)


def _dot_nt(a, b):
    return lax.dot_general(a, b, (((1,), (1,)), ((), ())), preferred_element_type=F32)


def _dot_tn(a, b):
    return lax.dot_general(a, b, (((0,), (0,)), ((), ())), preferred_element_type=F32)


def _resident(shape, index_map):
    return pl.BlockSpec(shape, index_map, pipeline_mode=pl.Buffered(1))


def _ffn_kernel(x_hbm, g_ref, wg_ref, wu_ref, wd_ref, *rest):
    if len(rest) == 7:
        gnext_ref, h_hbm, n_ref, acc_ref, xn_ref, in_sem, out_sem = rest
    else:
        (h_hbm, acc_ref, xn_ref, in_sem, out_sem), gnext_ref, n_ref = rest, None, None
    i, f = pl.program_id(0), pl.program_id(1)
    nt, nf = pl.num_programs(0), pl.num_programs(1)
    tm = xn_ref.shape[0]
    slot = i % 2

    def x_copy(tile, s):
        return pltpu.make_async_copy(x_hbm.at[pl.ds(tile * tm, tm)], acc_ref.at[s], in_sem.at[s])

    def h_copy(tile, s):
        return pltpu.make_async_copy(acc_ref.at[s], h_hbm.at[pl.ds(tile * tm, tm)], out_sem.at[s])

    @pl.when(f == 0)
    def _():
        @pl.when(i == 0)
        def _():
            x_copy(0, 0).start()

        x_copy(i, slot).wait()
        xn_ref[...] = _rms(acc_ref[slot], g_ref[...]).astype(BF16)

    @pl.when(f == 1)
    def _():
        @pl.when(i > 0)
        def _():
            h_copy(i - 1, 1 - slot).wait()

        @pl.when(i + 1 < nt)
        def _():
            x_copy(i + 1, 1 - slot).start()

    xn = xn_ref[...]
    gate = _dot(xn, wg_ref[...].astype(BF16))
    up = _dot(xn, wu_ref[...].astype(BF16))
    act = ((0.5 * gate) * jax.nn.sigmoid(gate) * up).astype(BF16)
    acc_ref[slot] += _dot(act, wd_ref[...].astype(BF16))

    @pl.when(f == nf - 1)
    def _():
        if n_ref is not None:
            n_ref[...] = _rms(acc_ref[slot], gnext_ref[...]).astype(BF16)
        h_copy(i, slot).start()

        @pl.when(i == nt - 1)
        def _():
            h_copy(i, slot).wait()


def _ffn(x, gain, wg, wu, wd, gain_next, *, tm, tf):
    t, d = x.shape
    dff = wg.shape[1]
    assert dff // tf >= 2, "the accumulator hand-over happens at the second F step"
    emit = gain_next is not None
    gain_spec = pl.BlockSpec((1, d), lambda i, f: (0, 0))
    return pl.pallas_call(
        _ffn_kernel,
        grid=(t // tm, dff // tf),
        in_specs=[
            pl.BlockSpec(memory_space=pl.ANY),
            gain_spec,
            pl.BlockSpec((d, tf), lambda i, f: (0, f)),
            pl.BlockSpec((d, tf), lambda i, f: (0, f)),
            pl.BlockSpec((tf, d), lambda i, f: (f, 0)),
        ] + [gain_spec] * emit,
        out_specs=[pl.BlockSpec(memory_space=pl.ANY)] + [pl.BlockSpec((tm, d), lambda i, f: (i, 0))] * emit,
        out_shape=[jax.ShapeDtypeStruct((t, d), F32)] + [jax.ShapeDtypeStruct((t, d), BF16)] * emit,
        scratch_shapes=[
            pltpu.VMEM((2, tm, d), F32),
            pltpu.VMEM((tm, d), BF16),
            pltpu.SemaphoreType.DMA((2,)),
            pltpu.SemaphoreType.DMA((2,)),
        ],
        compiler_params=pltpu.CompilerParams(
            dimension_semantics=("arbitrary", "arbitrary"),
            vmem_limit_bytes=FFN_VMEM_LIMIT_BYTES),
        name="ffn",
    )(x, gain, wg, wu, wd, *([gain_next] * emit))


def _bcast_row(x, blk, r):
    c, l = x.shape
    x3 = x.reshape(c // blk, blk, l)
    return jnp.broadcast_to(x3[:, r:r + 1, :], (c // blk, blk, l)).reshape(c, l)


def _chunk_cumsum(x, tri):
    c, w = tri.shape[0], x.shape[1]
    hi = x.astype(BF16)
    r1 = x - hi.astype(F32)
    mid = r1.astype(BF16)
    lo = (r1 - mid.astype(F32)).astype(BF16)
    parts = jnp.concatenate([hi, mid, lo], axis=1)
    cs = jnp.concatenate([_dot(tri, parts[c0:c0 + c]) for c0 in range(0, x.shape[0], c)], axis=0)
    return cs[:, 0:w] + cs[:, w:2 * w] + cs[:, 2 * w:3 * w]


def _split_distance(b2, l2, m, sub):
    c = b2.shape[0]
    if m == 2:
        return jnp.where(sub % 2 == 1, l2, 0.0)
    if m == 4:
        return b2 - jnp.where(sub < 4, _bcast_row(b2, SUBLANES, 1), _bcast_row(b2, SUBLANES, 5))
    if m == SUBLANES:
        return b2 - _bcast_row(b2, SUBLANES, 3)
    half = m // 2
    return jnp.concatenate(
        [b2[r0:r0 + m] - b2[r0 + half - 1:r0 + half] for r0 in range(0, c, m)], axis=0)


def _hgrn_tile(p_ref, r0, ts, lb, gn, state_t, o_ref, chunk, side_work, side_slots):
    starts = range(0, ts, chunk)
    rows = lambda x, c0: x[c0:c0 + chunk]
    q = p_ref[r0:r0 + ts, 0 * HEAD_DIM:1 * HEAD_DIM]
    fl = p_ref[r0:r0 + ts, 1 * HEAD_DIM:2 * HEAD_DIM]
    v = p_ref[r0:r0 + ts, 2 * HEAD_DIM:3 * HEAD_DIM].astype(BF16)
    g = p_ref[r0:r0 + ts, 3 * HEAD_DIM:4 * HEAD_DIM]

    tt = lax.broadcasted_iota(jnp.int32, (chunk, chunk), 0)
    ss = lax.broadcasted_iota(jnp.int32, (chunk, chunk), 1)
    tri = (ss <= tt).astype(BF16)
    diag = chunk.bit_length() - 1
    level = jnp.where(ss < tt, 31 - lax.clz(tt ^ ss), jnp.where(ss == tt, diag, -1))

    f = lb + (1.0 - lb) * jax.nn.sigmoid(fl)
    k = 1.0 - f
    qf = q * jax.nn.sigmoid(q)
    l2 = jnp.log2(f)
    def side(slot):
        for work, s in zip(side_work, side_slots):
            if s == slot:
                work()

    side(0)
    b2 = _chunk_cumsum(l2, tri)

    qb, kb = qf.astype(BF16), k.astype(BF16)
    qs = qb * jnp.exp2(b2).astype(BF16)
    to_end = jnp.concatenate([b2[c0 + chunk - 1:c0 + chunk] - rows(b2, c0) for c0 in starts], axis=0)
    kd = kb * jnp.exp2(to_end).astype(BF16)
    carried = []
    for c0 in starts:
        carried.append(_dot_nt(rows(qs, c0), state_t.astype(BF16)))
        state_t = state_t * jnp.exp2(b2[c0 + chunk - 1:c0 + chunk]) + _dot_tn(rows(v, c0), rows(kd, c0))

    sub = lax.broadcasted_iota(jnp.int32, (ts, HEAD_DIM), 0) % SUBLANES
    side(1)
    scores = [jnp.where(level == diag, _dot_nt(rows(qb, c0), rows(kb, c0)), 0.0) for c0 in starts]
    m, li = 2, 0
    while m <= chunk:
        e = jnp.exp2(-jnp.abs(_split_distance(b2, l2, m, sub))).astype(BF16)
        qe, ke = qb * e, kb * e
        side(2 + li)
        scores = [jnp.where(level == li, _dot_nt(rows(qe, c0), rows(ke, c0)), sc)
                  for c0, sc in zip(starts, scores)]
        m, li = 2 * m, li + 1

    o = [_dot(sc.astype(BF16), rows(v, c0)) + oc for c0, sc, oc in zip(starts, scores, carried)]
    side(2 + diag)
    o = jnp.concatenate(o, axis=0)
    o_ref[0, r0:r0 + ts, :] = (_rms(o, gn) * (g * jax.nn.sigmoid(g))).astype(o_ref.dtype)
    return state_t


def _hgrn_kernel(u_ref, wq_ref, wf_ref, wi_ref, wg_ref, lbl_ref, gn_ref, o_ref,
                 w_ref, pa_ref, pb_ref, state_ref, *, chunk, sub_rows, piece_rows, piece_slots, ni, n):
    j = pl.program_id(0)
    ts = pa_ref.shape[0]

    @pl.when(j == 0)
    def _():
        pb_ref[...] = jnp.zeros_like(pb_ref)
        state_ref[...] = jnp.zeros_like(state_ref)

    @pl.when((j % ni == 0) & (j < n))
    def _():
        for sec, src in enumerate((wq_ref, wf_ref, wi_ref, wg_ref)):
            w_ref[:, sec * HEAD_DIM:(sec + 1) * HEAD_DIM] = src[...].astype(BF16)

    def step(p_new_ref, p_ref):
        def piece(r0, c0):
            p_new_ref[r0:r0 + piece_rows, c0:c0 + MXU_COLS] = _dot(
                u_ref[0, r0:r0 + piece_rows, :], w_ref[:, c0:c0 + MXU_COLS])

        pieces = [functools.partial(piece, r0, c0)
                  for r0 in range(0, ts, piece_rows) for c0 in range(0, 4 * HEAD_DIM, MXU_COLS)]

        logits = [lbl_ref[l, 0] for l in range(lbl_ref.shape[0])]
        mx = functools.reduce(jnp.maximum, logits)
        ex = [jnp.exp(l - mx) for l in logits]
        lb = ex[0] / functools.reduce(jnp.add, ex)

        first = (jnp.maximum(j - 1, 0) % ni) == 0
        state_t = jnp.where(first, 0.0, state_ref[...])
        n_sub = ts // sub_rows
        per = len(pieces) // n_sub
        for si in range(n_sub):
            state_t = _hgrn_tile(p_ref, si * sub_rows, sub_rows, lb, gn_ref[...], state_t, o_ref, chunk,
                                 pieces[si * per:(si + 1) * per], piece_slots)
        state_ref[...] = state_t

    @pl.when(j % 2 == 0)
    def _():
        step(pa_ref, pb_ref)

    @pl.when(j % 2 == 1)
    def _():
        step(pb_ref, pa_ref)


def _hgrn(u, w_in, lb_logits, gn, *, heads, ts, chunk, sub_rows, piece_rows, piece_slots):
    bsz, s, d = u.shape
    nl = lb_logits.shape[0]
    ni = s // ts
    n = bsz * heads * ni

    def cur(j):
        jc = jnp.minimum(j, n - 1)
        return jc // (heads * ni), (jc // ni) % heads, jc % ni

    def prev(j):
        return cur(jnp.maximum(j - 1, 0))

    w_spec = lambda sec: pl.BlockSpec((d, HEAD_DIM), lambda j: (0, sec * heads + cur(j)[1]))
    return pl.pallas_call(
        functools.partial(_hgrn_kernel, chunk=chunk, sub_rows=sub_rows, piece_rows=piece_rows,
                          piece_slots=piece_slots, ni=ni, n=n),
        grid=(n + 1,),
        in_specs=[
            pl.BlockSpec((1, ts, d), lambda j: (cur(j)[0], cur(j)[2], 0)),
            w_spec(0), w_spec(1), w_spec(2), w_spec(3),
            pl.BlockSpec((nl, 1, 1, HEAD_DIM), lambda j: (0, prev(j)[1], 0, 0)),
            pl.BlockSpec((1, HEAD_DIM), lambda j: (0, 0)),
        ],
        out_specs=pl.BlockSpec((1, ts, HEAD_DIM), lambda j: (prev(j)[0], prev(j)[2], prev(j)[1])),
        out_shape=jax.ShapeDtypeStruct((bsz, s, heads * HEAD_DIM), BF16),
        scratch_shapes=[
            pltpu.VMEM((d, 4 * HEAD_DIM), BF16),
            pltpu.VMEM((ts, 4 * HEAD_DIM), F32),
            pltpu.VMEM((ts, 4 * HEAD_DIM), F32),
            pltpu.VMEM((HEAD_DIM, HEAD_DIM), F32),
        ],
        compiler_params=pltpu.CompilerParams(
            dimension_semantics=("arbitrary",),
            vmem_limit_bytes=VMEM_LIMIT_BYTES),
        name="hgrn",
    )(u, w_in, w_in, w_in, w_in, lb_logits, gn)


def _conv_kernel(u_ref, wb_ref, wc_ref, wv_ref, cw_ref, gn_ref, o_ref,
                 w_ref, pa_ref, pb_ref, halo_ref, *, ni, sweep, n):
    j = pl.program_id(0)
    ts, gw = o_ref.shape[1], o_ref.shape[2]

    @pl.when(j == 0)
    def _():
        pb_ref[...] = jnp.zeros_like(pb_ref)
        halo_ref[...] = jnp.zeros_like(halo_ref)

    @pl.when((j % sweep == 0) & (j < n))
    def _():
        for sec, src in enumerate((wb_ref, wc_ref, wv_ref)):
            w_ref[:, sec * gw:(sec + 1) * gw] = src[...].astype(BF16)

    def step(p_new_ref, p_ref):
        p_new_ref[...] = _dot(u_ref[0], w_ref[...])

        first = (jnp.maximum(j - 1, 0) % ni) == 0
        bg = p_ref[:, 0:gw]
        uc = p_ref[:, gw:2 * gw] * p_ref[:, 2 * gw:3 * gw]
        row = lax.broadcasted_iota(jnp.int32, (ts, gw), 0)
        prev1 = jnp.where(first, 0.0, halo_ref[1:2, :])
        prev2 = jnp.where(first, 0.0, halo_ref[0:1, :])
        u1 = jnp.where(row >= 1, pltpu.roll(uc, 1, axis=0), prev1)
        u2 = jnp.where(row >= 2, pltpu.roll(uc, 2, axis=0), jnp.where(row == 1, prev1, prev2))
        halo_ref[...] = uc[ts - 2:ts, :]
        y = bg * (cw_ref[0:1, :] * u2 + cw_ref[1:2, :] * u1 + cw_ref[2:3, :] * uc)
        for g0 in range(0, gw, HEAD_DIM):
            yg = y[:, g0:g0 + HEAD_DIM]
            o_ref[0, :, g0:g0 + HEAD_DIM] = _rms(yg, gn_ref[:, g0:g0 + HEAD_DIM]).astype(o_ref.dtype)

    @pl.when(j % 2 == 0)
    def _():
        step(pa_ref, pb_ref)

    @pl.when(j % 2 == 1)
    def _():
        step(pb_ref, pa_ref)


def _conv(u, w_in, conv_w, gn, *, col0, ts, gw):
    bsz, s, d = u.shape
    cw = conv_w.shape[1]
    ni = s // ts
    sweep = bsz * ni
    n = (cw // gw) * sweep

    def cur(j):
        jc = jnp.minimum(j, n - 1)
        return jc // sweep, (jc // ni) % bsz, jc % ni

    def prev(j):
        return cur(jnp.maximum(j - 1, 0))

    w_spec = lambda sec: pl.BlockSpec((d, gw), lambda j: (0, (col0 + sec * cw) // gw + cur(j)[0]))
    return pl.pallas_call(
        functools.partial(_conv_kernel, ni=ni, sweep=sweep, n=n),
        grid=(n + 1,),
        in_specs=[
            pl.BlockSpec((1, ts, d), lambda j: (cur(j)[1], cur(j)[2], 0)),
            w_spec(0), w_spec(1), w_spec(2),
            pl.BlockSpec((3, gw), lambda j: (0, prev(j)[0])),
            pl.BlockSpec((1, gw), lambda j: (0, prev(j)[0])),
        ],
        out_specs=pl.BlockSpec((1, ts, gw), lambda j: (prev(j)[1], prev(j)[2], prev(j)[0])),
        out_shape=jax.ShapeDtypeStruct((bsz, s, cw), BF16),
        scratch_shapes=[
            pltpu.VMEM((d, 3 * gw), BF16),
            pltpu.VMEM((ts, 3 * gw), F32),
            pltpu.VMEM((ts, 3 * gw), F32),
            pltpu.VMEM((2, gw), F32),
        ],
        compiler_params=pltpu.CompilerParams(
            dimension_semantics=("arbitrary",),
            vmem_limit_bytes=VMEM_LIMIT_BYTES),
        name="conv",
    )(u, w_in, w_in, w_in, conv_w, gn)


def _outproj_kernel(h_ref, a_ref, b_ref, wo_ref, o_ref, w_ref):
    @pl.when(pl.program_id(0) == 0)
    def _():
        w_ref[...] = wo_ref[...].astype(BF16)

    ka = a_ref.shape[1]
    o_ref[...] = h_ref[...] + _dot(a_ref[...], w_ref[0:ka, :]) + _dot(b_ref[...], w_ref[ka:, :])


def _outproj(h, a, b, wo, *, tm):
    t, d = h.shape
    ka, kb = a.shape[1], b.shape[1]
    return pl.pallas_call(
        _outproj_kernel,
        grid=(t // tm,),
        in_specs=[
            pl.BlockSpec((tm, d), lambda i: (i, 0)),
            pl.BlockSpec((tm, ka), lambda i: (i, 0)),
            pl.BlockSpec((tm, kb), lambda i: (i, 0)),
            _resident((ka + kb, d), lambda i: (0, 0)),
        ],
        out_specs=pl.BlockSpec((tm, d), lambda i: (i, 0)),
        out_shape=jax.ShapeDtypeStruct((t, d), F32),
        scratch_shapes=[pltpu.VMEM((ka + kb, d), BF16)],
        compiler_params=pltpu.CompilerParams(
            dimension_semantics=("arbitrary",),
            vmem_limit_bytes=VMEM_LIMIT_BYTES),
        name="outproj",
    )(h, a, b, wo)


def _ple_kernel(h_ref, gn_ref, p_ref, wg_ref, we_ref, gf_ref, o_ref, w_ref):
    @pl.when(pl.program_id(0) == 0)
    def _():
        w_ref[...] = wg_ref[...].astype(BF16)

    h = h_ref[...]
    gate = jax.nn.sigmoid(_dot(_rms(h, gn_ref[...]).astype(BF16), w_ref[...]))
    pe = _dot(p_ref[...].astype(BF16), we_ref[...].astype(BF16))
    o_ref[...] = _rms(h + gate * pe, gf_ref[...])


def _ple(h, gn, p, wg, we, gf, *, tm):
    t, d = h.shape
    pd = p.shape[1]
    return pl.pallas_call(
        _ple_kernel,
        grid=(t // tm,),
        in_specs=[
            pl.BlockSpec((tm, d), lambda i: (i, 0)),
            pl.BlockSpec((1, d), lambda i: (0, 0)),
            pl.BlockSpec((tm, pd), lambda i: (i, 0)),
            _resident((d, d), lambda i: (0, 0)),
            _resident((pd, d), lambda i: (0, 0)),
            pl.BlockSpec((1, d), lambda i: (0, 0)),
        ],
        out_specs=pl.BlockSpec((tm, d), lambda i: (i, 0)),
        out_shape=jax.ShapeDtypeStruct((t, d), F32),
        scratch_shapes=[pltpu.VMEM((d, d), BF16)],
        compiler_params=pltpu.CompilerParams(
            dimension_semantics=("arbitrary",),
            vmem_limit_bytes=VMEM_LIMIT_BYTES),
        name="ple",
    )(h, gn, p, wg, we, gf)


def kernel(x, p, norm_ffn1, ffn1_gate, ffn1_up, ffn1_down, norm_mix, w_in, conv_w, hgrn_lb_logits,
           hgrn_norm, conv_norm, w_out, norm_ffn2, ffn2_gate, ffn2_up, ffn2_down, norm_ple, w_ple,
           w_ple_gate, norm_final):
    bsz, s, d = x.shape
    depth = w_in.shape[0]
    assert depth == 1, "single-layer block"
    t = bsz * s
    hw = hgrn_lb_logits.shape[1]
    heads = hw // HEAD_DIM
    cw = conv_w.shape[2]
    assert w_in.shape[2] == 4 * hw + 3 * cw

    row = lambda a: a.reshape(1, -1)
    i = 0
    lb_logits = hgrn_lb_logits.reshape(-1, heads, 1, HEAD_DIM)

    h1, u = _ffn(x.reshape(t, d), row(norm_ffn1[i]), ffn1_gate[i], ffn1_up[i], ffn1_down[i],
                 row(norm_mix[i]), tm=1024, tf=512)
    u3 = u.reshape(bsz, s, d)
    o_h = _hgrn(u3, w_in[i], lb_logits, row(hgrn_norm[i]), heads=heads, ts=1024, chunk=128,
              sub_rows=512, piece_rows=128, piece_slots=(0, 0, 0, 2, 4, 6, 9, 9))
    y_c = _conv(u3, w_in[i], conv_w[i], row(conv_norm[i]), col0=4 * hw, ts=1024, gw=256)
    h2 = _outproj(h1, o_h.reshape(t, hw), y_c.reshape(t, cw), w_out[i], tm=512)
    (h3,) = _ffn(h2, row(norm_ffn2[i]), ffn2_gate[i], ffn2_up[i], ffn2_down[i], None, tm=1024, tf=512)
    y = _ple(h3, row(norm_ple[i]), p[i].reshape(t, -1), w_ple_gate[i], w_ple[i], row(norm_final), tm=512)
    return y.reshape(bsz, s, d)
```

```python
import functools

import jax
import jax.numpy as jnp
from jax import lax
from jax.experimental import pallas as pl
from jax.experimental.pallas import tpu as pltpu

EPS = 1e-6
HEAD_DIM = 128
SUBLANES = 8
MXU_COLS = 256
MAX_PLAIN_LOG2_DECAY = 96.0
VMEM_LIMIT_BYTES = 56 * 1024 * 1024
FFN_VMEM_LIMIT_BYTES = 60 * 1024 * 1024

F32 = jnp.float32
BF16 = jnp.bfloat16


def _rms(x, gain):
    r = lax.rsqrt(jnp.mean(x * x, axis=-1, keepdims=True) + EPS)
    return (x * r) * gain


def _dot(a, b):
    return jnp.dot(a, b, preferred_element_type=F32)


def _dot_nt(a, b):
    return lax.dot_general(a, b, (((1,), (1,)), ((), ())), preferred_element_type=F32)


def _dot_tn(a, b):
    return lax.dot_general(a, b, (((0,), (0,)), ((), ())), preferred_element_type=F32)


def _resident(shape, index_map):
    return pl.BlockSpec(shape, index_map, pipeline_mode=pl.Buffered(1))


def _ffn_kernel(x_hbm, g_ref, wg_ref, wu_ref, wd_ref, *rest):
    if len(rest) == 7:
        gnext_ref, h_hbm, n_ref, acc_ref, xn_ref, in_sem, out_sem = rest
    else:
        (h_hbm, acc_ref, xn_ref, in_sem, out_sem), gnext_ref, n_ref = rest, None, None
    i, f = pl.program_id(0), pl.program_id(1)
    nt, nf = pl.num_programs(0), pl.num_programs(1)
    tm = xn_ref.shape[0]
    slot = i % 2

    def x_copy(tile, s):
        return pltpu.make_async_copy(x_hbm.at[pl.ds(tile * tm, tm)], acc_ref.at[s], in_sem.at[s])

    def h_copy(tile, s):
        return pltpu.make_async_copy(acc_ref.at[s], h_hbm.at[pl.ds(tile * tm, tm)], out_sem.at[s])

    @pl.when(f == 0)
    def _():
        @pl.when(i == 0)
        def _():
            x_copy(0, 0).start()

        x_copy(i, slot).wait()
        xn_ref[...] = _rms(acc_ref[slot], g_ref[...]).astype(BF16)

    @pl.when(f == 1)
    def _():
        @pl.when(i > 0)
        def _():
            h_copy(i - 1, 1 - slot).wait()

        @pl.when(i + 1 < nt)
        def _():
            x_copy(i + 1, 1 - slot).start()

    xn = xn_ref[...]
    gate = _dot(xn, wg_ref[...].astype(BF16))
    up = _dot(xn, wu_ref[...].astype(BF16))
    act = ((0.5 * gate) * jax.nn.sigmoid(gate) * up).astype(BF16)
    acc_ref[slot] += _dot(act, wd_ref[...].astype(BF16))

    @pl.when(f == nf - 1)
    def _():
        if n_ref is not None:
            n_ref[...] = _rms(acc_ref[slot], gnext_ref[...]).astype(BF16)
        h_copy(i, slot).start()

        @pl.when(i == nt - 1)
        def _():
            h_copy(i, slot).wait()


def _ffn(x, gain, wg, wu, wd, gain_next, *, tm, tf):
    t, d = x.shape
    dff = wg.shape[1]
    assert dff // tf >= 2, "the accumulator hand-over happens at the second F step"
    emit = gain_next is not None
    gain_spec = pl.BlockSpec((1, d), lambda i, f: (0, 0))
    return pl.pallas_call(
        _ffn_kernel,
        grid=(t // tm, dff // tf),
        in_specs=[
            pl.BlockSpec(memory_space=pl.ANY),
            gain_spec,
            pl.BlockSpec((d, tf), lambda i, f: (0, f)),
            pl.BlockSpec((d, tf), lambda i, f: (0, f)),
            pl.BlockSpec((tf, d), lambda i, f: (f, 0)),
        ] + [gain_spec] * emit,
        out_specs=[pl.BlockSpec(memory_space=pl.ANY)] + [pl.BlockSpec((tm, d), lambda i, f: (i, 0))] * emit,
        out_shape=[jax.ShapeDtypeStruct((t, d), F32)] + [jax.ShapeDtypeStruct((t, d), BF16)] * emit,
        scratch_shapes=[
            pltpu.VMEM((2, tm, d), F32),
            pltpu.VMEM((tm, d), BF16),
            pltpu.SemaphoreType.DMA((2,)),
            pltpu.SemaphoreType.DMA((2,)),
        ],
        compiler_params=pltpu.CompilerParams(
            dimension_semantics=("arbitrary", "arbitrary"),
            vmem_limit_bytes=FFN_VMEM_LIMIT_BYTES),
        name="ffn",
    )(x, gain, wg, wu, wd, *([gain_next] * emit))


def _bcast_row(x, blk, r):
    c, l = x.shape
    x3 = x.reshape(c // blk, blk, l)
    return jnp.broadcast_to(x3[:, r:r + 1, :], (c // blk, blk, l)).reshape(c, l)


def _chunk_cumsum(x, tri):
    c, w = tri.shape[0], x.shape[1]
    hi = x.astype(BF16)
    r1 = x - hi.astype(F32)
    mid = r1.astype(BF16)
    lo = (r1 - mid.astype(F32)).astype(BF16)
    parts = jnp.concatenate([hi, mid, lo], axis=1)
    cs = jnp.concatenate([_dot(tri, parts[c0:c0 + c]) for c0 in range(0, x.shape[0], c)], axis=0)
    return cs[:, 0:w] + cs[:, w:2 * w] + cs[:, 2 * w:3 * w]


def _split_distance(b2, l2, m, sub):
    c = b2.shape[0]
    if m == 2:
        return jnp.where(sub % 2 == 1, l2, 0.0)
    if m == 4:
        return b2 - jnp.where(sub < 4, _bcast_row(b2, SUBLANES, 1), _bcast_row(b2, SUBLANES, 5))
    if m == SUBLANES:
        return b2 - _bcast_row(b2, SUBLANES, 3)
    half = m // 2
    return jnp.concatenate(
        [b2[r0:r0 + m] - b2[r0 + half - 1:r0 + half] for r0 in range(0, c, m)], axis=0)


def _hgrn_tile(p_ref, r0, ts, lb, gn, state_t, o_ref, scores_ref, chunk, side_work, side_slots):
    starts = range(0, ts, chunk)
    rows = lambda x, c0: x[c0:c0 + chunk]
    q = p_ref[r0:r0 + ts, 0 * HEAD_DIM:1 * HEAD_DIM]
    fl = p_ref[r0:r0 + ts, 1 * HEAD_DIM:2 * HEAD_DIM]
    v = p_ref[r0:r0 + ts, 2 * HEAD_DIM:3 * HEAD_DIM].astype(BF16)
    g = p_ref[r0:r0 + ts, 3 * HEAD_DIM:4 * HEAD_DIM]

    tt = lax.broadcasted_iota(jnp.int32, (chunk, chunk), 0)
    ss = lax.broadcasted_iota(jnp.int32, (chunk, chunk), 1)
    tri = (ss <= tt).astype(BF16)
    diag = chunk.bit_length() - 1
    level = jnp.where(ss < tt, 31 - lax.clz(tt ^ ss), jnp.where(ss == tt, diag, -1))

    f = lb + (1.0 - lb) * jax.nn.sigmoid(fl)
    k = 1.0 - f
    qf = q * jax.nn.sigmoid(q)
    l2 = jnp.log2(f)
    def side(slot):
        for work, s in zip(side_work, side_slots):
            if s == slot:
                work()

    side(0)
    b2 = _chunk_cumsum(l2, tri)

    qb, kb = qf.astype(BF16), k.astype(BF16)
    sub = lax.broadcasted_iota(jnp.int32, (ts, HEAD_DIM), 0) % SUBLANES
    side(1)

    def split_scores(scores, first_m, first_li):
        m, li = first_m, first_li
        while m <= chunk:
            e = jnp.exp2(-jnp.abs(_split_distance(b2, l2, m, sub))).astype(BF16)
            qe, ke = qb * e, kb * e
            scores = [jnp.where(level == li, _dot_nt(rows(qe, c0), rows(ke, c0)), sc)
                      for c0, sc in zip(starts, scores)]
            m, li = 2 * m, li + 1
        return scores

    half = chunk // 2
    d_mid = jnp.concatenate(
        [b2[h0:h0 + half] - b2[h0 + half // 2 - 1:h0 + half // 2] for h0 in range(0, ts, half)], axis=0)
    near = jnp.max(jnp.abs(d_mid)) < MAX_PLAIN_LOG2_DECAY

    def emit(scores):
        for c0, sc in zip(starts, scores):
            scores_ref[c0:c0 + chunk, :] = sc.astype(BF16)

    qe, ke = qb * jnp.exp2(d_mid).astype(BF16), kb * jnp.exp2(-d_mid).astype(BF16)
    inside = (level >= 0) & (level != diag - 1)
    emit(split_scores([jnp.where(inside, _dot_nt(rows(qe, c0), rows(ke, c0)), 0.0) for c0 in starts],
                      chunk, diag - 1))
    side(2)

    @pl.when(jnp.logical_not(near))
    def _():
        emit(split_scores(
            [jnp.where(level == diag, _dot_nt(rows(qb, c0), rows(kb, c0)), 0.0) for c0 in starts], 2, 0))

    qs = qb * jnp.exp2(b2).astype(BF16)
    to_end = jnp.concatenate([b2[c0 + chunk - 1:c0 + chunk] - rows(b2, c0) for c0 in starts], axis=0)
    kd = kb * jnp.exp2(to_end).astype(BF16)
    o = []
    for c0 in starts:
        o.append(_dot(scores_ref[c0:c0 + chunk, :], rows(v, c0)) + _dot_nt(rows(qs, c0), state_t.astype(BF16)))
        state_t = state_t * jnp.exp2(b2[c0 + chunk - 1:c0 + chunk]) + _dot_tn(rows(v, c0), rows(kd, c0))
    side(3)
    o = jnp.concatenate(o, axis=0)
    o_ref[0, r0:r0 + ts, :] = (_rms(o, gn) * (g * jax.nn.sigmoid(g))).astype(o_ref.dtype)
    return state_t


def _hgrn_kernel(u_ref, wq_ref, wf_ref, wi_ref, wg_ref, lbl_ref, gn_ref, o_ref,
                 w_ref, pa_ref, pb_ref, state_ref, scores_ref, *, chunk, sub_rows, piece_rows, piece_slots, ni, n):
    j = pl.program_id(0)
    ts = pa_ref.shape[0]

    @pl.when(j == 0)
    def _():
        pb_ref[...] = jnp.zeros_like(pb_ref)
        state_ref[...] = jnp.zeros_like(state_ref)

    @pl.when((j % ni == 0) & (j < n))
    def _():
        for sec, src in enumerate((wq_ref, wf_ref, wi_ref, wg_ref)):
            w_ref[:, sec * HEAD_DIM:(sec + 1) * HEAD_DIM] = src[...].astype(BF16)

    def step(p_new_ref, p_ref):
        def piece(r0, c0):
            p_new_ref[r0:r0 + piece_rows, c0:c0 + MXU_COLS] = _dot(
                u_ref[0, r0:r0 + piece_rows, :], w_ref[:, c0:c0 + MXU_COLS])

        pieces = [functools.partial(piece, r0, c0)
                  for r0 in range(0, ts, piece_rows) for c0 in range(0, 4 * HEAD_DIM, MXU_COLS)]

        logits = [lbl_ref[l, 0] for l in range(lbl_ref.shape[0])]
        mx = functools.reduce(jnp.maximum, logits)
        ex = [jnp.exp(l - mx) for l in logits]
        lb = ex[0] / functools.reduce(jnp.add, ex)

        first = (jnp.maximum(j - 1, 0) % ni) == 0
        state_t = jnp.where(first, 0.0, state_ref[...])
        n_sub = ts // sub_rows
        per = len(pieces) // n_sub
        for si in range(n_sub):
            state_t = _hgrn_tile(p_ref, si * sub_rows, sub_rows, lb, gn_ref[...], state_t, o_ref, scores_ref, chunk,
                                 pieces[si * per:(si + 1) * per], piece_slots)
        state_ref[...] = state_t

    @pl.when(j % 2 == 0)
    def _():
        step(pa_ref, pb_ref)

    @pl.when(j % 2 == 1)
    def _():
        step(pb_ref, pa_ref)


def _hgrn(u, w_in, lb_logits, gn, *, heads, ts, chunk, sub_rows, piece_rows, piece_slots):
    bsz, s, d = u.shape
    nl = lb_logits.shape[0]
    ni = s // ts
    n = bsz * heads * ni

    def cur(j):
        jc = jnp.minimum(j, n - 1)
        return jc // (heads * ni), (jc // ni) % heads, jc % ni

    def prev(j):
        return cur(jnp.maximum(j - 1, 0))

    w_spec = lambda sec: pl.BlockSpec((d, HEAD_DIM), lambda j: (0, sec * heads + cur(j)[1]))
    return pl.pallas_call(
        functools.partial(_hgrn_kernel, chunk=chunk, sub_rows=sub_rows, piece_rows=piece_rows,
                          piece_slots=piece_slots, ni=ni, n=n),
        grid=(n + 1,),
        in_specs=[
            pl.BlockSpec((1, ts, d), lambda j: (cur(j)[0], cur(j)[2], 0)),
            w_spec(0), w_spec(1), w_spec(2), w_spec(3),
            pl.BlockSpec((nl, 1, 1, HEAD_DIM), lambda j: (0, prev(j)[1], 0, 0)),
            pl.BlockSpec((1, HEAD_DIM), lambda j: (0, 0)),
        ],
        out_specs=pl.BlockSpec((1, ts, HEAD_DIM), lambda j: (prev(j)[0], prev(j)[2], prev(j)[1])),
        out_shape=jax.ShapeDtypeStruct((bsz, s, heads * HEAD_DIM), BF16),
        scratch_shapes=[
            pltpu.VMEM((d, 4 * HEAD_DIM), BF16),
            pltpu.VMEM((ts, 4 * HEAD_DIM), F32),
            pltpu.VMEM((ts, 4 * HEAD_DIM), F32),
            pltpu.VMEM((HEAD_DIM, HEAD_DIM), F32),
            pltpu.VMEM((sub_rows, chunk), BF16),
        ],
        compiler_params=pltpu.CompilerParams(
            dimension_semantics=("arbitrary",),
            vmem_limit_bytes=VMEM_LIMIT_BYTES),
        name="hgrn",
    )(u, w_in, w_in, w_in, w_in, lb_logits, gn)


def _conv_kernel(u_ref, wb_ref, wc_ref, wv_ref, cw_ref, gn_ref, o_ref,
                 w_ref, pa_ref, pb_ref, halo_ref, *, ni, sweep, n):
    j = pl.program_id(0)
    ts, gw = o_ref.shape[1], o_ref.shape[2]

    @pl.when(j == 0)
    def _():
        pb_ref[...] = jnp.zeros_like(pb_ref)
        halo_ref[...] = jnp.zeros_like(halo_ref)

    @pl.when((j % sweep == 0) & (j < n))
    def _():
        for sec, src in enumerate((wb_ref, wc_ref, wv_ref)):
            w_ref[:, sec * gw:(sec + 1) * gw] = src[...].astype(BF16)

    def step(p_new_ref, p_ref):
        p_new_ref[...] = _dot(u_ref[0], w_ref[...])

        first = (jnp.maximum(j - 1, 0) % ni) == 0
        bg = p_ref[:, 0:gw]
        uc = p_ref[:, gw:2 * gw] * p_ref[:, 2 * gw:3 * gw]
        row = lax.broadcasted_iota(jnp.int32, (ts, gw), 0)
        prev1 = jnp.where(first, 0.0, halo_ref[1:2, :])
        prev2 = jnp.where(first, 0.0, halo_ref[0:1, :])
        u1 = jnp.where(row >= 1, pltpu.roll(uc, 1, axis=0), prev1)
        u2 = jnp.where(row >= 2, pltpu.roll(uc, 2, axis=0), jnp.where(row == 1, prev1, prev2))
        halo_ref[...] = uc[ts - 2:ts, :]
        y = bg * (cw_ref[0:1, :] * u2 + cw_ref[1:2, :] * u1 + cw_ref[2:3, :] * uc)
        for g0 in range(0, gw, HEAD_DIM):
            yg = y[:, g0:g0 + HEAD_DIM]
            o_ref[0, :, g0:g0 + HEAD_DIM] = _rms(yg, gn_ref[:, g0:g0 + HEAD_DIM]).astype(o_ref.dtype)

    @pl.when(j % 2 == 0)
    def _():
        step(pa_ref, pb_ref)

    @pl.when(j % 2 == 1)
    def _():
        step(pb_ref, pa_ref)


def _conv(u, w_in, conv_w, gn, *, col0, ts, gw):
    bsz, s, d = u.shape
    cw = conv_w.shape[1]
    ni = s // ts
    sweep = bsz * ni
    n = (cw // gw) * sweep

    def cur(j):
        jc = jnp.minimum(j, n - 1)
        return jc // sweep, (jc // ni) % bsz, jc % ni

    def prev(j):
        return cur(jnp.maximum(j - 1, 0))

    w_spec = lambda sec: pl.BlockSpec((d, gw), lambda j: (0, (col0 + sec * cw) // gw + cur(j)[0]))
    return pl.pallas_call(
        functools.partial(_conv_kernel, ni=ni, sweep=sweep, n=n),
        grid=(n + 1,),
        in_specs=[
            pl.BlockSpec((1, ts, d), lambda j: (cur(j)[1], cur(j)[2], 0)),
            w_spec(0), w_spec(1), w_spec(2),
            pl.BlockSpec((3, gw), lambda j: (0, prev(j)[0])),
            pl.BlockSpec((1, gw), lambda j: (0, prev(j)[0])),
        ],
        out_specs=pl.BlockSpec((1, ts, gw), lambda j: (prev(j)[1], prev(j)[2], prev(j)[0])),
        out_shape=jax.ShapeDtypeStruct((bsz, s, cw), BF16),
        scratch_shapes=[
            pltpu.VMEM((d, 3 * gw), BF16),
            pltpu.VMEM((ts, 3 * gw), F32),
            pltpu.VMEM((ts, 3 * gw), F32),
            pltpu.VMEM((2, gw), F32),
        ],
        compiler_params=pltpu.CompilerParams(
            dimension_semantics=("arbitrary",),
            vmem_limit_bytes=VMEM_LIMIT_BYTES),
        name="conv",
    )(u, w_in, w_in, w_in, conv_w, gn)


def _outproj_kernel(h_ref, a_ref, b_ref, wo_ref, o_ref, w_ref):
    @pl.when(pl.program_id(0) == 0)
    def _():
        w_ref[...] = wo_ref[...].astype(BF16)

    ka = a_ref.shape[1]
    o_ref[...] = h_ref[...] + _dot(a_ref[...], w_ref[0:ka, :]) + _dot(b_ref[...], w_ref[ka:, :])


def _outproj(h, a, b, wo, *, tm):
    t, d = h.shape
    ka, kb = a.shape[1], b.shape[1]
    return pl.pallas_call(
        _outproj_kernel,
        grid=(t // tm,),
        in_specs=[
            pl.BlockSpec((tm, d), lambda i: (i, 0)),
            pl.BlockSpec((tm, ka), lambda i: (i, 0)),
            pl.BlockSpec((tm, kb), lambda i: (i, 0)),
            _resident((ka + kb, d), lambda i: (0, 0)),
        ],
        out_specs=pl.BlockSpec((tm, d), lambda i: (i, 0)),
        out_shape=jax.ShapeDtypeStruct((t, d), F32),
        scratch_shapes=[pltpu.VMEM((ka + kb, d), BF16)],
        compiler_params=pltpu.CompilerParams(
            dimension_semantics=("arbitrary",),
            vmem_limit_bytes=VMEM_LIMIT_BYTES),
        name="outproj",
    )(h, a, b, wo)


def _ple_kernel(h_ref, gn_ref, p_ref, wg_ref, we_ref, gf_ref, o_ref, w_ref):
    @pl.when(pl.program_id(0) == 0)
    def _():
        w_ref[...] = wg_ref[...].astype(BF16)

    h = h_ref[...]
    gate = jax.nn.sigmoid(_dot(_rms(h, gn_ref[...]).astype(BF16), w_ref[...]))
    pe = _dot(p_ref[...].astype(BF16), we_ref[...].astype(BF16))
    o_ref[...] = _rms(h + gate * pe, gf_ref[...])


def _ple(h, gn, p, wg, we, gf, *, tm):
    t, d = h.shape
    pd = p.shape[1]
    return pl.pallas_call(
        _ple_kernel,
        grid=(t // tm,),
        in_specs=[
            pl.BlockSpec((tm, d), lambda i: (i, 0)),
            pl.BlockSpec((1, d), lambda i: (0, 0)),
            pl.BlockSpec((tm, pd), lambda i: (i, 0)),
            _resident((d, d), lambda i: (0, 0)),
            _resident((pd, d), lambda i: (0, 0)),
            pl.BlockSpec((1, d), lambda i: (0, 0)),
        ],
        out_specs=pl.BlockSpec((tm, d), lambda i: (i, 0)),
        out_shape=jax.ShapeDtypeStruct((t, d), F32),
        scratch_shapes=[pltpu.VMEM((d, d), BF16)],
        compiler_params=pltpu.CompilerParams(
            dimension_semantics=("arbitrary",),
            vmem_limit_bytes=VMEM_LIMIT_BYTES),
        name="ple",
    )(h, gn, p, wg, we, gf)


def kernel(x, p, norm_ffn1, ffn1_gate, ffn1_up, ffn1_down, norm_mix, w_in, conv_w, hgrn_lb_logits,
           hgrn_norm, conv_norm, w_out, norm_ffn2, ffn2_gate, ffn2_up, ffn2_down, norm_ple, w_ple,
           w_ple_gate, norm_final):
    bsz, s, d = x.shape
    depth = w_in.shape[0]
    assert depth == 1, "single-layer block"
    t = bsz * s
    hw = hgrn_lb_logits.shape[1]
    heads = hw // HEAD_DIM
    cw = conv_w.shape[2]
    assert w_in.shape[2] == 4 * hw + 3 * cw

    row = lambda a: a.reshape(1, -1)
    i = 0
    lb_logits = hgrn_lb_logits.reshape(-1, heads, 1, HEAD_DIM)

    h1, u = _ffn(x.reshape(t, d), row(norm_ffn1[i]), ffn1_gate[i], ffn1_up[i], ffn1_down[i],
                 row(norm_mix[i]), tm=1024, tf=512)
    u3 = u.reshape(bsz, s, d)
    o_h = _hgrn(u3, w_in[i], lb_logits, row(hgrn_norm[i]), heads=heads, ts=1024, chunk=128,
              sub_rows=512, piece_rows=128, piece_slots=(0, 0, 1, 1, 2, 2, 3, 3))
    y_c = _conv(u3, w_in[i], conv_w[i], row(conv_norm[i]), col0=4 * hw, ts=1024, gw=256)
    h2 = _outproj(h1, o_h.reshape(t, hw), y_c.reshape(t, cw), w_out[i], tm=512)
    (h3,) = _ffn(h2, row(norm_ffn2[i]), ffn2_gate[i], ffn2_up[i], ffn2_down[i], None, tm=1024, tf=512)
    y = _ple(h3, row(norm_ple[i]), p[i].reshape(t, -1), w_ple_gate[i], w_ple[i], row(norm_final), tm=512)
    return y.reshape(bsz, s, d)
```

```python
import functools

import jax
import jax.numpy as jnp
from jax import lax
from jax.experimental import pallas as pl
from jax.experimental.pallas import tpu as pltpu

EPS = 1e-6
HEAD_DIM = 128
SUBLANES = 8
MXU_COLS = 256
MAX_PLAIN_LOG2_DECAY = 96.0
VMEM_LIMIT_BYTES = 56 * 1024 * 1024
FFN_VMEM_LIMIT_BYTES = 60 * 1024 * 1024

F32 = jnp.float32
BF16 = jnp.bfloat16


def _rms(x, gain):
    r = lax.rsqrt(jnp.mean(x * x, axis=-1, keepdims=True) + EPS)
    return (x * r) * gain


def _dot(a, b):
    return jnp.dot(a, b, preferred_element_type=F32)


def _dot_nt(a, b):
    return lax.dot_general(a, b, (((1,), (1,)), ((), ())), preferred_element_type=F32)


def _dot_tn(a, b):
    return lax.dot_general(a, b, (((0,), (0,)), ((), ())), preferred_element_type=F32)


def _resident(shape, index_map):
    return pl.BlockSpec(shape, index_map, pipeline_mode=pl.Buffered(1))


def _ffn_kernel(x_hbm, g_ref, wg_ref, wu_ref, wd_ref, *rest):
    if len(rest) == 7:
        gnext_ref, h_hbm, n_ref, acc_ref, xn_ref, in_sem, out_sem = rest
    else:
        (h_hbm, acc_ref, xn_ref, in_sem, out_sem), gnext_ref, n_ref = rest, None, None
    i, f = pl.program_id(0), pl.program_id(1)
    nt, nf = pl.num_programs(0), pl.num_programs(1)
    tm = xn_ref.shape[0]
    slot = i % 2

    def x_copy(tile, s):
        return pltpu.make_async_copy(x_hbm.at[pl.ds(tile * tm, tm)], acc_ref.at[s], in_sem.at[s])

    def h_copy(tile, s):
        return pltpu.make_async_copy(acc_ref.at[s], h_hbm.at[pl.ds(tile * tm, tm)], out_sem.at[s])

    @pl.when(f == 0)
    def _():
        @pl.when(i == 0)
        def _():
            x_copy(0, 0).start()

        x_copy(i, slot).wait()
        xn_ref[...] = _rms(acc_ref[slot], g_ref[...]).astype(BF16)

    @pl.when(f == 1)
    def _():
        @pl.when(i > 0)
        def _():
            h_copy(i - 1, 1 - slot).wait()

        @pl.when(i + 1 < nt)
        def _():
            x_copy(i + 1, 1 - slot).start()

    xn = xn_ref[...]
    gate = _dot(xn, wg_ref[...].astype(BF16))
    up = _dot(xn, wu_ref[...].astype(BF16))
    act = ((0.5 * gate) * jax.nn.sigmoid(gate) * up).astype(BF16)
    acc_ref[slot] += _dot(act, wd_ref[...].astype(BF16))

    @pl.when(f == nf - 1)
    def _():
        if n_ref is not None:
            n_ref[...] = _rms(acc_ref[slot], gnext_ref[...]).astype(BF16)
        h_copy(i, slot).start()

        @pl.when(i == nt - 1)
        def _():
            h_copy(i, slot).wait()


def _ffn(x, gain, wg, wu, wd, gain_next, *, tm, tf):
    t, d = x.shape
    dff = wg.shape[1]
    assert dff // tf >= 2, "the accumulator hand-over happens at the second F step"
    emit = gain_next is not None
    gain_spec = pl.BlockSpec((1, d), lambda i, f: (0, 0))
    return pl.pallas_call(
        _ffn_kernel,
        grid=(t // tm, dff // tf),
        in_specs=[
            pl.BlockSpec(memory_space=pl.ANY),
            gain_spec,
            pl.BlockSpec((d, tf), lambda i, f: (0, f)),
            pl.BlockSpec((d, tf), lambda i, f: (0, f)),
            pl.BlockSpec((tf, d), lambda i, f: (f, 0)),
        ] + [gain_spec] * emit,
        out_specs=[pl.BlockSpec(memory_space=pl.ANY)] + [pl.BlockSpec((tm, d), lambda i, f: (i, 0))] * emit,
        out_shape=[jax.ShapeDtypeStruct((t, d), F32)] + [jax.ShapeDtypeStruct((t, d), BF16)] * emit,
        scratch_shapes=[
            pltpu.VMEM((2, tm, d), F32),
            pltpu.VMEM((tm, d), BF16),
            pltpu.SemaphoreType.DMA((2,)),
            pltpu.SemaphoreType.DMA((2,)),
        ],
        compiler_params=pltpu.CompilerParams(
            dimension_semantics=("arbitrary", "arbitrary"),
            vmem_limit_bytes=FFN_VMEM_LIMIT_BYTES),
        name="ffn",
    )(x, gain, wg, wu, wd, *([gain_next] * emit))


def _bcast_row(x, blk, r):
    c, l = x.shape
    x3 = x.reshape(c // blk, blk, l)
    return jnp.broadcast_to(x3[:, r:r + 1, :], (c // blk, blk, l)).reshape(c, l)


def _chunk_cumsum(x, tri):
    c, w = tri.shape[0], x.shape[1]
    hi = x.astype(BF16)
    r1 = x - hi.astype(F32)
    mid = r1.astype(BF16)
    lo = (r1 - mid.astype(F32)).astype(BF16)
    parts = jnp.concatenate([hi, mid, lo], axis=1)
    cs = jnp.concatenate([_dot(tri, parts[c0:c0 + c]) for c0 in range(0, x.shape[0], c)], axis=0)
    return cs[:, 0:w] + cs[:, w:2 * w] + cs[:, 2 * w:3 * w]


def _split_distance(b2, l2, m, sub):
    c = b2.shape[0]
    if m == 2:
        return jnp.where(sub % 2 == 1, l2, 0.0)
    if m == 4:
        return b2 - jnp.where(sub < 4, _bcast_row(b2, SUBLANES, 1), _bcast_row(b2, SUBLANES, 5))
    if m == SUBLANES:
        return b2 - _bcast_row(b2, SUBLANES, 3)
    half = m // 2
    return jnp.concatenate(
        [b2[r0:r0 + m] - b2[r0 + half - 1:r0 + half] for r0 in range(0, c, m)], axis=0)


def _hgrn_tile(p_ref, r0, ts, lb, gn, state_t, o_ref, scores_ref, chunk, side_work, side_slots):
    starts = range(0, ts, chunk)
    rows = lambda x, c0: x[c0:c0 + chunk]
    q = p_ref[r0:r0 + ts, 0 * HEAD_DIM:1 * HEAD_DIM]
    fl = p_ref[r0:r0 + ts, 1 * HEAD_DIM:2 * HEAD_DIM]
    v = p_ref[r0:r0 + ts, 2 * HEAD_DIM:3 * HEAD_DIM].astype(BF16)
    g = p_ref[r0:r0 + ts, 3 * HEAD_DIM:4 * HEAD_DIM]

    tt = lax.broadcasted_iota(jnp.int32, (chunk, chunk), 0)
    ss = lax.broadcasted_iota(jnp.int32, (chunk, chunk), 1)
    tri = (ss <= tt).astype(BF16)
    diag = chunk.bit_length() - 1
    level = jnp.where(ss < tt, 31 - lax.clz(tt ^ ss), jnp.where(ss == tt, diag, -1))

    f = lb + (1.0 - lb) * jax.nn.sigmoid(fl)
    k = 1.0 - f
    qf = q * jax.nn.sigmoid(q)
    l2 = jnp.log2(f)
    def side(slot):
        for work, s in zip(side_work, side_slots):
            if s == slot:
                work()

    side(0)
    b2 = _chunk_cumsum(l2, tri)

    qb, kb = qf.astype(BF16), k.astype(BF16)
    sub = lax.broadcasted_iota(jnp.int32, (ts, HEAD_DIM), 0) % SUBLANES
    side(1)

    def split_scores(scores, first_m, first_li):
        m, li = first_m, first_li
        while m <= chunk:
            e = jnp.exp2(-jnp.abs(_split_distance(b2, l2, m, sub))).astype(BF16)
            qe, ke = qb * e, kb * e
            scores = [jnp.where(level == li, _dot_nt(rows(qe, c0), rows(ke, c0)), sc)
                      for c0, sc in zip(starts, scores)]
            m, li = 2 * m, li + 1
        return scores

    half = chunk // 2
    d_mid = jnp.concatenate(
        [b2[h0:h0 + half] - b2[h0 + half // 2 - 1:h0 + half // 2] for h0 in range(0, ts, half)], axis=0)
    near = jnp.max(jnp.abs(d_mid)) < MAX_PLAIN_LOG2_DECAY

    def emit(scores):
        for c0, sc in zip(starts, scores):
            scores_ref[c0:c0 + chunk, :] = sc.astype(BF16)

    qe, ke = qb * jnp.exp2(d_mid).astype(BF16), kb * jnp.exp2(-d_mid).astype(BF16)
    inside = (level >= 0) & (level != diag - 1)
    emit(split_scores([jnp.where(inside, _dot_nt(rows(qe, c0), rows(ke, c0)), 0.0) for c0 in starts],
                      chunk, diag - 1))
    side(2)

    @pl.when(jnp.logical_not(near))
    def _():
        emit(split_scores(
            [jnp.where(level == diag, _dot_nt(rows(qb, c0), rows(kb, c0)), 0.0) for c0 in starts], 2, 0))

    qs = qb * jnp.exp2(b2).astype(BF16)
    to_end = jnp.concatenate([b2[c0 + chunk - 1:c0 + chunk] - rows(b2, c0) for c0 in starts], axis=0)
    kd = kb * jnp.exp2(to_end).astype(BF16)
    o = []
    for c0 in starts:
        o.append(_dot(scores_ref[c0:c0 + chunk, :], rows(v, c0)) + _dot_nt(rows(qs, c0), state_t.astype(BF16)))
        state_t = state_t * jnp.exp2(b2[c0 + chunk - 1:c0 + chunk]) + _dot_tn(rows(v, c0), rows(kd, c0))
    side(3)
    o = jnp.concatenate(o, axis=0)
    o_ref[0, r0:r0 + ts, :] = (_rms(o, gn) * (g * jax.nn.sigmoid(g))).astype(o_ref.dtype)
    return state_t


def _hgrn_kernel(u_ref, wq_ref, wf_ref, wi_ref, wg_ref, lbl_ref, gn_ref, o_ref,
                 w_ref, pa_ref, pb_ref, state_ref, scores_ref, *, chunk, sub_rows, piece_rows, piece_slots, ni, n):
    j = pl.program_id(0)
    ts = pa_ref.shape[0]

    @pl.when(j == 0)
    def _():
        pb_ref[...] = jnp.zeros_like(pb_ref)
        state_ref[...] = jnp.zeros_like(state_ref)

    @pl.when((j % ni == 0) & (j < n))
    def _():
        for sec, src in enumerate((wq_ref, wf_ref, wi_ref, wg_ref)):
            w_ref[:, sec * HEAD_DIM:(sec + 1) * HEAD_DIM] = src[...].astype(BF16)

    def step(p_new_ref, p_ref):
        def piece(r0, c0):
            p_new_ref[r0:r0 + piece_rows, c0:c0 + MXU_COLS] = _dot(
                u_ref[0, r0:r0 + piece_rows, :], w_ref[:, c0:c0 + MXU_COLS])

        pieces = [functools.partial(piece, r0, c0)
                  for r0 in range(0, ts, piece_rows) for c0 in range(0, 4 * HEAD_DIM, MXU_COLS)]

        logits = [lbl_ref[l, 0] for l in range(lbl_ref.shape[0])]
        mx = functools.reduce(jnp.maximum, logits)
        ex = [jnp.exp(l - mx) for l in logits]
        lb = ex[0] / functools.reduce(jnp.add, ex)

        first = (jnp.maximum(j - 1, 0) % ni) == 0
        state_t = jnp.where(first, 0.0, state_ref[...])
        n_sub = ts // sub_rows
        per = len(pieces) // n_sub
        for si in range(n_sub):
            state_t = _hgrn_tile(p_ref, si * sub_rows, sub_rows, lb, gn_ref[...], state_t, o_ref, scores_ref, chunk,
                                 pieces[si * per:(si + 1) * per], piece_slots)
        state_ref[...] = state_t

    @pl.when(j % 2 == 0)
    def _():
        step(pa_ref, pb_ref)

    @pl.when(j % 2 == 1)
    def _():
        step(pb_ref, pa_ref)


def _hgrn(u, w_in, lb_logits, gn, *, heads, ts, chunk, sub_rows, piece_rows, piece_slots):
    bsz, s, d = u.shape
    nl = lb_logits.shape[0]
    ni = s // ts
    n = bsz * heads * ni

    def cur(j):
        jc = jnp.minimum(j, n - 1)
        return jc // (heads * ni), (jc // ni) % heads, jc % ni

    def prev(j):
        return cur(jnp.maximum(j - 1, 0))

    w_spec = lambda sec: pl.BlockSpec((d, HEAD_DIM), lambda j: (0, sec * heads + cur(j)[1]))
    return pl.pallas_call(
        functools.partial(_hgrn_kernel, chunk=chunk, sub_rows=sub_rows, piece_rows=piece_rows,
                          piece_slots=piece_slots, ni=ni, n=n),
        grid=(n + 1,),
        in_specs=[
            pl.BlockSpec((1, ts, d), lambda j: (cur(j)[0], cur(j)[2], 0)),
            w_spec(0), w_spec(1), w_spec(2), w_spec(3),
            pl.BlockSpec((nl, 1, 1, HEAD_DIM), lambda j: (0, prev(j)[1], 0, 0)),
            pl.BlockSpec((1, HEAD_DIM), lambda j: (0, 0)),
        ],
        out_specs=pl.BlockSpec((1, ts, HEAD_DIM), lambda j: (prev(j)[0], prev(j)[2], prev(j)[1])),
        out_shape=jax.ShapeDtypeStruct((bsz, s, heads * HEAD_DIM), BF16),
        scratch_shapes=[
            pltpu.VMEM((d, 4 * HEAD_DIM), BF16),
            pltpu.VMEM((ts, 4 * HEAD_DIM), F32),
            pltpu.VMEM((ts, 4 * HEAD_DIM), F32),
            pltpu.VMEM((HEAD_DIM, HEAD_DIM), F32),
            pltpu.VMEM((sub_rows, chunk), BF16),
        ],
        compiler_params=pltpu.CompilerParams(
            dimension_semantics=("arbitrary",),
            vmem_limit_bytes=VMEM_LIMIT_BYTES),
        name="hgrn",
    )(u, w_in, w_in, w_in, w_in, lb_logits, gn)


def _conv_kernel(u_ref, wb_ref, wc_ref, wv_ref, cw_ref, gn_ref, o_ref,
                 w_ref, pa_ref, pb_ref, halo_ref, *, ni, sweep, n):
    j = pl.program_id(0)
    ts, gw = o_ref.shape[1], o_ref.shape[2]

    @pl.when(j == 0)
    def _():
        pb_ref[...] = jnp.zeros_like(pb_ref)
        halo_ref[...] = jnp.zeros_like(halo_ref)

    @pl.when((j % sweep == 0) & (j < n))
    def _():
        for sec, src in enumerate((wb_ref, wc_ref, wv_ref)):
            w_ref[:, sec * gw:(sec + 1) * gw] = src[...].astype(BF16)

    def step(p_new_ref, p_ref):
        p_new_ref[...] = _dot(u_ref[0], w_ref[...])

        first = (jnp.maximum(j - 1, 0) % ni) == 0
        bg = p_ref[:, 0:gw]
        uc = p_ref[:, gw:2 * gw] * p_ref[:, 2 * gw:3 * gw]
        row = lax.broadcasted_iota(jnp.int32, (ts, gw), 0)
        prev1 = jnp.where(first, 0.0, halo_ref[1:2, :])
        prev2 = jnp.where(first, 0.0, halo_ref[0:1, :])
        u1 = jnp.where(row >= 1, pltpu.roll(uc, 1, axis=0), prev1)
        u2 = jnp.where(row >= 2, pltpu.roll(uc, 2, axis=0), jnp.where(row == 1, prev1, prev2))
        halo_ref[...] = uc[ts - 2:ts, :]
        y = bg * (cw_ref[0:1, :] * u2 + cw_ref[1:2, :] * u1 + cw_ref[2:3, :] * uc)
        for g0 in range(0, gw, HEAD_DIM):
            yg = y[:, g0:g0 + HEAD_DIM]
            o_ref[0, :, g0:g0 + HEAD_DIM] = _rms(yg, gn_ref[:, g0:g0 + HEAD_DIM]).astype(o_ref.dtype)

    @pl.when(j % 2 == 0)
    def _():
        step(pa_ref, pb_ref)

    @pl.when(j % 2 == 1)
    def _():
        step(pb_ref, pa_ref)


def _conv(u, w_in, conv_w, gn, *, col0, ts, gw):
    bsz, s, d = u.shape
    cw = conv_w.shape[1]
    ni = s // ts
    sweep = bsz * ni
    n = (cw // gw) * sweep

    def cur(j):
        jc = jnp.minimum(j, n - 1)
        return jc // sweep, (jc // ni) % bsz, jc % ni

    def prev(j):
        return cur(jnp.maximum(j - 1, 0))

    w_spec = lambda sec: pl.BlockSpec((d, gw), lambda j: (0, (col0 + sec * cw) // gw + cur(j)[0]))
    return pl.pallas_call(
        functools.partial(_conv_kernel, ni=ni, sweep=sweep, n=n),
        grid=(n + 1,),
        in_specs=[
            pl.BlockSpec((1, ts, d), lambda j: (cur(j)[1], cur(j)[2], 0)),
            w_spec(0), w_spec(1), w_spec(2),
            pl.BlockSpec((3, gw), lambda j: (0, prev(j)[0])),
            pl.BlockSpec((1, gw), lambda j: (0, prev(j)[0])),
        ],
        out_specs=pl.BlockSpec((1, ts, gw), lambda j: (prev(j)[1], prev(j)[2], prev(j)[0])),
        out_shape=jax.ShapeDtypeStruct((bsz, s, cw), BF16),
        scratch_shapes=[
            pltpu.VMEM((d, 3 * gw), BF16),
            pltpu.VMEM((ts, 3 * gw), F32),
            pltpu.VMEM((ts, 3 * gw), F32),
            pltpu.VMEM((2, gw), F32),
        ],
        compiler_params=pltpu.CompilerParams(
            dimension_semantics=("arbitrary",),
            vmem_limit_bytes=VMEM_LIMIT_BYTES),
        name="conv",
    )(u, w_in, w_in, w_in, conv_w, gn)


def _outproj_kernel(h_ref, a_ref, b_ref, wo_ref, o_ref, w_ref):
    @pl.when(pl.program_id(0) == 0)
    def _():
        w_ref[...] = wo_ref[...].astype(BF16)

    ka = a_ref.shape[1]
    o_ref[...] = h_ref[...] + _dot(a_ref[...], w_ref[0:ka, :]) + _dot(b_ref[...], w_ref[ka:, :])


def _outproj(h, a, b, wo, *, tm):
    t, d = h.shape
    ka, kb = a.shape[1], b.shape[1]
    return pl.pallas_call(
        _outproj_kernel,
        grid=(t // tm,),
        in_specs=[
            pl.BlockSpec((tm, d), lambda i: (i, 0)),
            pl.BlockSpec((tm, ka), lambda i: (i, 0)),
            pl.BlockSpec((tm, kb), lambda i: (i, 0)),
            _resident((ka + kb, d), lambda i: (0, 0)),
        ],
        out_specs=pl.BlockSpec((tm, d), lambda i: (i, 0)),
        out_shape=jax.ShapeDtypeStruct((t, d), F32),
        scratch_shapes=[pltpu.VMEM((ka + kb, d), BF16)],
        compiler_params=pltpu.CompilerParams(
            dimension_semantics=("arbitrary",),
            vmem_limit_bytes=VMEM_LIMIT_BYTES),
        name="outproj",
    )(h, a, b, wo)


def _ple_kernel(h_ref, gn_ref, p_ref, wg_ref, we_ref, gf_ref, o_ref, w_ref):
    @pl.when(pl.program_id(0) == 0)
    def _():
        w_ref[...] = wg_ref[...].astype(BF16)

    h = h_ref[...]
    gate = jax.nn.sigmoid(_dot(_rms(h, gn_ref[...]).astype(BF16), w_ref[...]))
    pe = _dot(p_ref[...].astype(BF16), we_ref[...].astype(BF16))
    o_ref[...] = _rms(h + gate * pe, gf_ref[...])


def _ple(h, gn, p, wg, we, gf, *, tm):
    t, d = h.shape
    pd = p.shape[1]
    return pl.pallas_call(
        _ple_kernel,
        grid=(t // tm,),
        in_specs=[
            pl.BlockSpec((tm, d), lambda i: (i, 0)),
            pl.BlockSpec((1, d), lambda i: (0, 0)),
            pl.BlockSpec((tm, pd), lambda i: (i, 0)),
            _resident((d, d), lambda i: (0, 0)),
            _resident((pd, d), lambda i: (0, 0)),
            pl.BlockSpec((1, d), lambda i: (0, 0)),
        ],
        out_specs=pl.BlockSpec((tm, d), lambda i: (i, 0)),
        out_shape=jax.ShapeDtypeStruct((t, d), F32),
        scratch_shapes=[pltpu.VMEM((d, d), BF16)],
        compiler_params=pltpu.CompilerParams(
            dimension_semantics=("arbitrary",),
            vmem_limit_bytes=VMEM_LIMIT_BYTES),
        name="ple",
    )(h, gn, p, wg, we, gf)


def kernel(x, p, norm_ffn1, ffn1_gate, ffn1_up, ffn1_down, norm_mix, w_in, conv_w, hgrn_lb_logits,
           hgrn_norm, conv_norm, w_out, norm_ffn2, ffn2_gate, ffn2_up, ffn2_down, norm_ple, w_ple,
           w_ple_gate, norm_final):
    bsz, s, d = x.shape
    depth = w_in.shape[0]
    assert depth == 1, "single-layer block"
    t = bsz * s
    hw = hgrn_lb_logits.shape[1]
    heads = hw // HEAD_DIM
    cw = conv_w.shape[2]
    assert w_in.shape[2] == 4 * hw + 3 * cw

    row = lambda a: a.reshape(1, -1)
    i = 0
    lb_logits = hgrn_lb_logits.reshape(-1, heads, 1, HEAD_DIM)

    h1, u = _ffn(x.reshape(t, d), row(norm_ffn1[i]), ffn1_gate[i], ffn1_up[i], ffn1_down[i],
                 row(norm_mix[i]), tm=1024, tf=512)
    u3 = u.reshape(bsz, s, d)
    o_h = _hgrn(u3, w_in[i], lb_logits, row(hgrn_norm[i]), heads=heads, ts=1024, chunk=128,
              sub_rows=512, piece_rows=512, piece_slots=(0, 3))
    y_c = _conv(u3, w_in[i], conv_w[i], row(conv_norm[i]), col0=4 * hw, ts=1024, gw=256)
    h2 = _outproj(h1, o_h.reshape(t, hw), y_c.reshape(t, cw), w_out[i], tm=512)
    (h3,) = _ffn(h2, row(norm_ffn2[i]), ffn2_gate[i], ffn2_up[i], ffn2_down[i], None, tm=1024, tf=512)
    y = _ple(h3, row(norm_ple[i]), p[i].reshape(t, -1), w_ple_gate[i], w_ple[i], row(norm_final), tm=512)
    return y.reshape(bsz, s, d)
```

```python
import functools
from typing import NamedTuple

import jax
import jax.numpy as jnp
from jax import lax
from jax.experimental import pallas as pl
from jax.experimental.pallas import tpu as pltpu

EPS = 1e-6
HEAD_DIM = 128
SUBLANES = 8
MAX_PLAIN_LOG2_DECAY = 96.0
VMEM_LIMIT_BYTES = 56 * 1024 * 1024
FFN_VMEM_LIMIT_BYTES = 60 * 1024 * 1024


class _Tiles(NamedTuple):
    ffn_tokens: int = 1024
    ffn_hidden: int = 512
    hgrn_seq: int = 2048
    hgrn_sub: int = 512
    hgrn_chunk: int = 128
    hgrn_piece: tuple = (512, 512)
    hgrn_piece_slots: tuple = (1,)
    conv_seq: int = 1024
    conv_channels: int = 256
    proj_tokens: int = 512


TILES = _Tiles()

F32 = jnp.float32
BF16 = jnp.bfloat16


def _rms(x, gain):
    r = lax.rsqrt(jnp.mean(x * x, axis=-1, keepdims=True) + EPS)
    return (x * r) * gain


def _dot(a, b):
    return jnp.dot(a, b, preferred_element_type=F32)


def _dot_nt(a, b):
    return lax.dot_general(a, b, (((1,), (1,)), ((), ())), preferred_element_type=F32)


def _dot_tn(a, b):
    return lax.dot_general(a, b, (((0,), (0,)), ((), ())), preferred_element_type=F32)


def _resident(shape, index_map):
    return pl.BlockSpec(shape, index_map, pipeline_mode=pl.Buffered(1))


def _ffn_kernel(x_hbm, g_ref, wg_ref, wu_ref, wd_ref, *rest):
    if len(rest) == 7:
        gnext_ref, h_hbm, n_ref, acc_ref, xn_ref, in_sem, out_sem = rest
    else:
        (h_hbm, acc_ref, xn_ref, in_sem, out_sem), gnext_ref, n_ref = rest, None, None
    i, f = pl.program_id(0), pl.program_id(1)
    nt, nf = pl.num_programs(0), pl.num_programs(1)
    tm = xn_ref.shape[0]
    slot = i % 2

    def x_copy(tile, s):
        return pltpu.make_async_copy(x_hbm.at[pl.ds(tile * tm, tm)], acc_ref.at[s], in_sem.at[s])

    def h_copy(tile, s):
        return pltpu.make_async_copy(acc_ref.at[s], h_hbm.at[pl.ds(tile * tm, tm)], out_sem.at[s])

    @pl.when(f == 0)
    def _():
        @pl.when(i == 0)
        def _():
            x_copy(0, 0).start()

        x_copy(i, slot).wait()
        xn_ref[...] = _rms(acc_ref[slot], g_ref[...]).astype(BF16)

    @pl.when(f == 1)
    def _():
        @pl.when(i > 0)
        def _():
            h_copy(i - 1, 1 - slot).wait()

        @pl.when(i + 1 < nt)
        def _():
            x_copy(i + 1, 1 - slot).start()

    xn = xn_ref[...]
    gate = _dot(xn, wg_ref[...].astype(BF16))
    up = _dot(xn, wu_ref[...].astype(BF16))
    act = ((0.5 * gate) * jax.nn.sigmoid(gate) * up).astype(BF16)
    acc_ref[slot] += _dot(act, wd_ref[...].astype(BF16))

    @pl.when(f == nf - 1)
    def _():
        if n_ref is not None:
            n_ref[...] = _rms(acc_ref[slot], gnext_ref[...]).astype(BF16)
        h_copy(i, slot).start()

        @pl.when(i == nt - 1)
        def _():
            h_copy(i, slot).wait()


def _ffn(x, gain, wg, wu, wd, gain_next, *, tm, tf):
    t, d = x.shape
    dff = wg.shape[1]
    assert dff // tf >= 2, "the accumulator hand-over happens at the second F step"
    emit = gain_next is not None
    gain_spec = pl.BlockSpec((1, d), lambda i, f: (0, 0))
    return pl.pallas_call(
        _ffn_kernel,
        grid=(t // tm, dff // tf),
        in_specs=[
            pl.BlockSpec(memory_space=pl.ANY),
            gain_spec,
            pl.BlockSpec((d, tf), lambda i, f: (0, f)),
            pl.BlockSpec((d, tf), lambda i, f: (0, f)),
            pl.BlockSpec((tf, d), lambda i, f: (f, 0)),
        ] + [gain_spec] * emit,
        out_specs=[pl.BlockSpec(memory_space=pl.ANY)] + [pl.BlockSpec((tm, d), lambda i, f: (i, 0))] * emit,
        out_shape=[jax.ShapeDtypeStruct((t, d), F32)] + [jax.ShapeDtypeStruct((t, d), BF16)] * emit,
        scratch_shapes=[
            pltpu.VMEM((2, tm, d), F32),
            pltpu.VMEM((tm, d), BF16),
            pltpu.SemaphoreType.DMA((2,)),
            pltpu.SemaphoreType.DMA((2,)),
        ],
        compiler_params=pltpu.CompilerParams(
            dimension_semantics=("arbitrary", "arbitrary"),
            vmem_limit_bytes=FFN_VMEM_LIMIT_BYTES),
        name="ffn",
    )(x, gain, wg, wu, wd, *([gain_next] * emit))


def _bcast_row(x, blk, r):
    c, l = x.shape
    x3 = x.reshape(c // blk, blk, l)
    return jnp.broadcast_to(x3[:, r:r + 1, :], (c // blk, blk, l)).reshape(c, l)


def _chunk_cumsum(x, tri):
    c, w = tri.shape[0], x.shape[1]
    hi = x.astype(BF16)
    r1 = x - hi.astype(F32)
    mid = r1.astype(BF16)
    lo = (r1 - mid.astype(F32)).astype(BF16)
    parts = jnp.concatenate([hi, mid, lo], axis=1)
    cs = jnp.concatenate([_dot(tri, parts[c0:c0 + c]) for c0 in range(0, x.shape[0], c)], axis=0)
    return cs[:, 0:w] + cs[:, w:2 * w] + cs[:, 2 * w:3 * w]


def _split_distance(b2, l2, m, sub):
    c = b2.shape[0]
    if m == 2:
        return jnp.where(sub % 2 == 1, l2, 0.0)
    if m == 4:
        return b2 - jnp.where(sub < 4, _bcast_row(b2, SUBLANES, 1), _bcast_row(b2, SUBLANES, 5))
    if m == SUBLANES:
        return b2 - _bcast_row(b2, SUBLANES, 3)
    half = m // 2
    return jnp.concatenate(
        [b2[r0:r0 + m] - b2[r0 + half - 1:r0 + half] for r0 in range(0, c, m)], axis=0)


def _hgrn_tile(p_ref, r0, ts, lb, gn, state_t, o_ref, chunk, side_work, side_slots, every_split):
    starts = range(0, ts, chunk)
    rows = lambda x, c0: x[c0:c0 + chunk]
    q = p_ref[r0:r0 + ts, 0 * HEAD_DIM:1 * HEAD_DIM]
    fl = p_ref[r0:r0 + ts, 1 * HEAD_DIM:2 * HEAD_DIM]
    v = p_ref[r0:r0 + ts, 2 * HEAD_DIM:3 * HEAD_DIM].astype(BF16)
    g = p_ref[r0:r0 + ts, 3 * HEAD_DIM:4 * HEAD_DIM]

    tt = lax.broadcasted_iota(jnp.int32, (chunk, chunk), 0)
    ss = lax.broadcasted_iota(jnp.int32, (chunk, chunk), 1)
    tri = (ss <= tt).astype(BF16)
    diag = chunk.bit_length() - 1
    level = jnp.where(ss < tt, 31 - lax.clz(tt ^ ss), jnp.where(ss == tt, diag, -1))

    f = lb + (1.0 - lb) * jax.nn.sigmoid(fl)
    k = 1.0 - f
    qf = q * jax.nn.sigmoid(q)
    l2 = jnp.log2(f)
    def side(slot):
        for work, s in zip(side_work, side_slots):
            if s == slot:
                work()

    side(0)
    b2 = _chunk_cumsum(l2, tri)

    qb, kb = qf.astype(BF16), k.astype(BF16)
    sub = lax.broadcasted_iota(jnp.int32, (ts, HEAD_DIM), 0) % SUBLANES
    side(1)

    def split_scores(scores, first_m, first_li):
        m, li = first_m, first_li
        while m <= chunk:
            e = jnp.exp2(-jnp.abs(_split_distance(b2, l2, m, sub))).astype(BF16)
            qe, ke = qb * e, kb * e
            scores = [jnp.where(level == li, _dot_nt(rows(qe, c0), rows(ke, c0)), sc)
                      for c0, sc in zip(starts, scores)]
            m, li = 2 * m, li + 1
        return scores

    half = chunk // 2
    d_mid = jnp.concatenate(
        [b2[h0:h0 + half] - b2[h0 + half // 2 - 1:h0 + half // 2] for h0 in range(0, ts, half)], axis=0)
    near = jnp.max(jnp.abs(d_mid)) < MAX_PLAIN_LOG2_DECAY
    if every_split:
        scores = split_scores(
            [jnp.where(level == diag, _dot_nt(rows(qb, c0), rows(kb, c0)), 0.0) for c0 in starts], 2, 0)
    else:
        qe, ke = qb * jnp.exp2(d_mid).astype(BF16), kb * jnp.exp2(-d_mid).astype(BF16)
        inside = (level >= 0) & (level != diag - 1)
        scores = split_scores([jnp.where(inside, _dot_nt(rows(qe, c0), rows(ke, c0)), 0.0) for c0 in starts],
                              chunk, diag - 1)
    side(2)

    qs = qb * jnp.exp2(b2).astype(BF16)
    to_end = jnp.concatenate([b2[c0 + chunk - 1:c0 + chunk] - rows(b2, c0) for c0 in starts], axis=0)
    kd = kb * jnp.exp2(to_end).astype(BF16)
    o = []
    for c0, sc in zip(starts, scores):
        o.append(_dot(sc.astype(BF16), rows(v, c0)) + _dot_nt(rows(qs, c0), state_t.astype(BF16)))
        state_t = state_t * jnp.exp2(b2[c0 + chunk - 1:c0 + chunk]) + _dot_tn(rows(v, c0), rows(kd, c0))
    side(3)
    o = jnp.concatenate(o, axis=0)
    o_ref[0, r0:r0 + ts, :] = (_rms(o, gn) * (g * jax.nn.sigmoid(g))).astype(o_ref.dtype)
    return state_t, near


def _hgrn_kernel(u_ref, wq_ref, wf_ref, wi_ref, wg_ref, lbl_ref, gn_ref, o_ref,
                 w_ref, pa_ref, pb_ref, state_ref, sub_state_ref, *, chunk, sub_rows, piece_rows, piece_cols, piece_slots,
                 ni, n):
    j = pl.program_id(0)
    ts = pa_ref.shape[0]

    @pl.when(j == 0)
    def _():
        pb_ref[...] = jnp.zeros_like(pb_ref)
        state_ref[...] = jnp.zeros_like(state_ref)

    @pl.when((j % ni == 0) & (j < n))
    def _():
        for sec, src in enumerate((wq_ref, wf_ref, wi_ref, wg_ref)):
            w_ref[:, sec * HEAD_DIM:(sec + 1) * HEAD_DIM] = src[...].astype(BF16)

    def step(p_new_ref, p_ref):
        def piece(r0, c0):
            p_new_ref[r0:r0 + piece_rows, c0:c0 + piece_cols] = _dot(
                u_ref[0, r0:r0 + piece_rows, :], w_ref[:, c0:c0 + piece_cols])

        pieces = [functools.partial(piece, r0, c0)
                  for r0 in range(0, ts, piece_rows) for c0 in range(0, 4 * HEAD_DIM, piece_cols)]

        logits = [lbl_ref[l, 0] for l in range(lbl_ref.shape[0])]
        mx = functools.reduce(jnp.maximum, logits)
        ex = [jnp.exp(l - mx) for l in logits]
        lb = ex[0] / functools.reduce(jnp.add, ex)

        first = (jnp.maximum(j - 1, 0) % ni) == 0
        state_t = jnp.where(first, 0.0, state_ref[...])
        n_sub = ts // sub_rows
        per = len(pieces) // n_sub
        all_near = True
        for si in range(n_sub):
            sub_state_ref[si] = state_t
            state_t, near = _hgrn_tile(p_ref, si * sub_rows, sub_rows, lb, gn_ref[...], state_t, o_ref, chunk,
                                       pieces[si * per:(si + 1) * per], piece_slots, every_split=False)
            all_near = jnp.logical_and(all_near, near)
        state_ref[...] = state_t

        @pl.when(jnp.logical_not(all_near))
        def _():
            for si in range(n_sub):
                _hgrn_tile(p_ref, si * sub_rows, sub_rows, lb, gn_ref[...], sub_state_ref[si], o_ref, chunk,
                           (), (), every_split=True)

    @pl.when(j % 2 == 0)
    def _():
        step(pa_ref, pb_ref)

    @pl.when(j % 2 == 1)
    def _():
        step(pb_ref, pa_ref)


def _hgrn(u, w_in, lb_logits, gn, *, heads, ts, chunk, sub_rows, piece_rows, piece_cols, piece_slots):
    bsz, s, d = u.shape
    nl = lb_logits.shape[0]
    ni = s // ts
    n = bsz * heads * ni

    def cur(j):
        jc = jnp.minimum(j, n - 1)
        return jc // (heads * ni), (jc // ni) % heads, jc % ni

    def prev(j):
        return cur(jnp.maximum(j - 1, 0))

    w_spec = lambda sec: pl.BlockSpec((d, HEAD_DIM), lambda j: (0, sec * heads + cur(j)[1]))
    return pl.pallas_call(
        functools.partial(_hgrn_kernel, chunk=chunk, sub_rows=sub_rows, piece_rows=piece_rows, piece_cols=piece_cols,
                          piece_slots=piece_slots, ni=ni, n=n),
        grid=(n + 1,),
        in_specs=[
            pl.BlockSpec((1, ts, d), lambda j: (cur(j)[0], cur(j)[2], 0)),
            w_spec(0), w_spec(1), w_spec(2), w_spec(3),
            pl.BlockSpec((nl, 1, 1, HEAD_DIM), lambda j: (0, prev(j)[1], 0, 0)),
            pl.BlockSpec((1, HEAD_DIM), lambda j: (0, 0)),
        ],
        out_specs=pl.BlockSpec((1, ts, HEAD_DIM), lambda j: (prev(j)[0], prev(j)[2], prev(j)[1])),
        out_shape=jax.ShapeDtypeStruct((bsz, s, heads * HEAD_DIM), BF16),
        scratch_shapes=[
            pltpu.VMEM((d, 4 * HEAD_DIM), BF16),
            pltpu.VMEM((ts, 4 * HEAD_DIM), F32),
            pltpu.VMEM((ts, 4 * HEAD_DIM), F32),
            pltpu.VMEM((HEAD_DIM, HEAD_DIM), F32),
            pltpu.VMEM((ts // sub_rows, HEAD_DIM, HEAD_DIM), F32),
        ],
        compiler_params=pltpu.CompilerParams(
            dimension_semantics=("arbitrary",),
            vmem_limit_bytes=VMEM_LIMIT_BYTES),
        name="hgrn",
    )(u, w_in, w_in, w_in, w_in, lb_logits, gn)


def _conv_kernel(u_ref, wb_ref, wc_ref, wv_ref, cw_ref, gn_ref, o_ref,
                 w_ref, pa_ref, pb_ref, halo_ref, *, ni, sweep, n):
    j = pl.program_id(0)
    ts, gw = o_ref.shape[1], o_ref.shape[2]

    @pl.when(j == 0)
    def _():
        pb_ref[...] = jnp.zeros_like(pb_ref)
        halo_ref[...] = jnp.zeros_like(halo_ref)

    @pl.when((j % sweep == 0) & (j < n))
    def _():
        for sec, src in enumerate((wb_ref, wc_ref, wv_ref)):
            w_ref[:, sec * gw:(sec + 1) * gw] = src[...].astype(BF16)

    def step(p_new_ref, p_ref):
        p_new_ref[...] = _dot(u_ref[0], w_ref[...])

        first = (jnp.maximum(j - 1, 0) % ni) == 0
        bg = p_ref[:, 0:gw]
        uc = p_ref[:, gw:2 * gw] * p_ref[:, 2 * gw:3 * gw]
        row = lax.broadcasted_iota(jnp.int32, (ts, gw), 0)
        prev1 = jnp.where(first, 0.0, halo_ref[1:2, :])
        prev2 = jnp.where(first, 0.0, halo_ref[0:1, :])
        u1 = jnp.where(row >= 1, pltpu.roll(uc, 1, axis=0), prev1)
        u2 = jnp.where(row >= 2, pltpu.roll(uc, 2, axis=0), jnp.where(row == 1, prev1, prev2))
        halo_ref[...] = uc[ts - 2:ts, :]
        y = bg * (cw_ref[0:1, :] * u2 + cw_ref[1:2, :] * u1 + cw_ref[2:3, :] * uc)
        for g0 in range(0, gw, HEAD_DIM):
            yg = y[:, g0:g0 + HEAD_DIM]
            o_ref[0, :, g0:g0 + HEAD_DIM] = _rms(yg, gn_ref[:, g0:g0 + HEAD_DIM]).astype(o_ref.dtype)

    @pl.when(j % 2 == 0)
    def _():
        step(pa_ref, pb_ref)

    @pl.when(j % 2 == 1)
    def _():
        step(pb_ref, pa_ref)


def _conv(u, w_in, conv_w, gn, *, col0, ts, gw):
    bsz, s, d = u.shape
    cw = conv_w.shape[1]
    ni = s // ts
    sweep = bsz * ni
    n = (cw // gw) * sweep

    def cur(j):
        jc = jnp.minimum(j, n - 1)
        return jc // sweep, (jc // ni) % bsz, jc % ni

    def prev(j):
        return cur(jnp.maximum(j - 1, 0))

    w_spec = lambda sec: pl.BlockSpec((d, gw), lambda j: (0, (col0 + sec * cw) // gw + cur(j)[0]))
    return pl.pallas_call(
        functools.partial(_conv_kernel, ni=ni, sweep=sweep, n=n),
        grid=(n + 1,),
        in_specs=[
            pl.BlockSpec((1, ts, d), lambda j: (cur(j)[1], cur(j)[2], 0)),
            w_spec(0), w_spec(1), w_spec(2),
            pl.BlockSpec((3, gw), lambda j: (0, prev(j)[0])),
            pl.BlockSpec((1, gw), lambda j: (0, prev(j)[0])),
        ],
        out_specs=pl.BlockSpec((1, ts, gw), lambda j: (prev(j)[1], prev(j)[2], prev(j)[0])),
        out_shape=jax.ShapeDtypeStruct((bsz, s, cw), BF16),
        scratch_shapes=[
            pltpu.VMEM((d, 3 * gw), BF16),
            pltpu.VMEM((ts, 3 * gw), F32),
            pltpu.VMEM((ts, 3 * gw), F32),
            pltpu.VMEM((2, gw), F32),
        ],
        compiler_params=pltpu.CompilerParams(
            dimension_semantics=("arbitrary",),
            vmem_limit_bytes=VMEM_LIMIT_BYTES),
        name="conv",
    )(u, w_in, w_in, w_in, conv_w, gn)


def _outproj_kernel(h_ref, a_ref, b_ref, wo_ref, o_ref, w_ref):
    @pl.when(pl.program_id(0) == 0)
    def _():
        w_ref[...] = wo_ref[...].astype(BF16)

    ka = a_ref.shape[1]
    o_ref[...] = h_ref[...] + _dot(a_ref[...], w_ref[0:ka, :]) + _dot(b_ref[...], w_ref[ka:, :])


def _outproj(h, a, b, wo, *, tm):
    t, d = h.shape
    ka, kb = a.shape[1], b.shape[1]
    return pl.pallas_call(
        _outproj_kernel,
        grid=(t // tm,),
        in_specs=[
            pl.BlockSpec((tm, d), lambda i: (i, 0)),
            pl.BlockSpec((tm, ka), lambda i: (i, 0)),
            pl.BlockSpec((tm, kb), lambda i: (i, 0)),
            _resident((ka + kb, d), lambda i: (0, 0)),
        ],
        out_specs=pl.BlockSpec((tm, d), lambda i: (i, 0)),
        out_shape=jax.ShapeDtypeStruct((t, d), F32),
        scratch_shapes=[pltpu.VMEM((ka + kb, d), BF16)],
        compiler_params=pltpu.CompilerParams(
            dimension_semantics=("arbitrary",),
            vmem_limit_bytes=VMEM_LIMIT_BYTES),
        name="outproj",
    )(h, a, b, wo)


def _ple_kernel(h_ref, gn_ref, p_ref, wg_ref, we_ref, gf_ref, o_ref, w_ref):
    @pl.when(pl.program_id(0) == 0)
    def _():
        w_ref[...] = wg_ref[...].astype(BF16)

    h = h_ref[...]
    gate = jax.nn.sigmoid(_dot(_rms(h, gn_ref[...]).astype(BF16), w_ref[...]))
    pe = _dot(p_ref[...].astype(BF16), we_ref[...].astype(BF16))
    o_ref[...] = _rms(h + gate * pe, gf_ref[...])


def _ple(h, gn, p, wg, we, gf, *, tm):
    t, d = h.shape
    pd = p.shape[1]
    return pl.pallas_call(
        _ple_kernel,
        grid=(t // tm,),
        in_specs=[
            pl.BlockSpec((tm, d), lambda i: (i, 0)),
            pl.BlockSpec((1, d), lambda i: (0, 0)),
            pl.BlockSpec((tm, pd), lambda i: (i, 0)),
            _resident((d, d), lambda i: (0, 0)),
            _resident((pd, d), lambda i: (0, 0)),
            pl.BlockSpec((1, d), lambda i: (0, 0)),
        ],
        out_specs=pl.BlockSpec((tm, d), lambda i: (i, 0)),
        out_shape=jax.ShapeDtypeStruct((t, d), F32),
        scratch_shapes=[pltpu.VMEM((d, d), BF16)],
        compiler_params=pltpu.CompilerParams(
            dimension_semantics=("arbitrary",),
            vmem_limit_bytes=VMEM_LIMIT_BYTES),
        name="ple",
    )(h, gn, p, wg, we, gf)


def kernel(x, p, norm_ffn1, ffn1_gate, ffn1_up, ffn1_down, norm_mix, w_in, conv_w, hgrn_lb_logits,
           hgrn_norm, conv_norm, w_out, norm_ffn2, ffn2_gate, ffn2_up, ffn2_down, norm_ple, w_ple,
           w_ple_gate, norm_final):
    bsz, s, d = x.shape
    depth = w_in.shape[0]
    assert depth == 1, "single-layer block"
    t = bsz * s
    hw = hgrn_lb_logits.shape[1]
    heads = hw // HEAD_DIM
    cw = conv_w.shape[2]
    assert w_in.shape[2] == 4 * hw + 3 * cw

    row = lambda a: a.reshape(1, -1)
    i = 0
    lb_logits = hgrn_lb_logits.reshape(-1, heads, 1, HEAD_DIM)

    tl = TILES
    h1, u = _ffn(x.reshape(t, d), row(norm_ffn1[i]), ffn1_gate[i], ffn1_up[i], ffn1_down[i],
                 row(norm_mix[i]), tm=tl.ffn_tokens, tf=tl.ffn_hidden)
    u3 = u.reshape(bsz, s, d)
    o_h = _hgrn(u3, w_in[i], lb_logits, row(hgrn_norm[i]), heads=heads, ts=tl.hgrn_seq, chunk=tl.hgrn_chunk,
                sub_rows=tl.hgrn_sub, piece_rows=tl.hgrn_piece[0], piece_cols=tl.hgrn_piece[1],
                piece_slots=tl.hgrn_piece_slots)
    y_c = _conv(u3, w_in[i], conv_w[i], row(conv_norm[i]), col0=4 * hw, ts=tl.conv_seq, gw=tl.conv_channels)
    h2 = _outproj(h1, o_h.reshape(t, hw), y_c.reshape(t, cw), w_out[i], tm=tl.proj_tokens)
    (h3,) = _ffn(h2, row(norm_ffn2[i]), ffn2_gate[i], ffn2_up[i], ffn2_down[i], None,
                 tm=tl.ffn_tokens, tf=tl.ffn_hidden)
    y = _ple(h3, row(norm_ple[i]), p[i].reshape(t, -1), w_ple_gate[i], w_ple[i], row(norm_final),
             tm=tl.proj_tokens)
    return y.reshape(bsz, s, d)
```

```python
import functools
from typing import NamedTuple

import jax
import jax.numpy as jnp
from jax import lax
from jax.experimental import pallas as pl
from jax.experimental.pallas import tpu as pltpu

EPS = 1e-6
HEAD_DIM = 128
SUBLANES = 8
MAX_PLAIN_LOG2_DECAY = 96.0
VMEM_LIMIT_BYTES = 56 * 1024 * 1024
FFN_VMEM_LIMIT_BYTES = 60 * 1024 * 1024


class _Tiles(NamedTuple):
    ffn_tokens: int = 1024
    ffn_hidden: int = 512
    hgrn_seq: int = 1024
    hgrn_sub: int = 512
    hgrn_chunk: int = 128
    hgrn_piece: tuple = (512, 512)
    hgrn_piece_slots: tuple = (1,)
    conv_seq: int = 1024
    conv_channels: int = 256
    proj_tokens: int = 512


TILES = _Tiles()

F32 = jnp.float32
BF16 = jnp.bfloat16


def _rms(x, gain):
    r = lax.rsqrt(jnp.mean(x * x, axis=-1, keepdims=True) + EPS)
    return (x * r) * gain


def _dot(a, b):
    return jnp.dot(a, b, preferred_element_type=F32)


def _dot_nt(a, b):
    return lax.dot_general(a, b, (((1,), (1,)), ((), ())), preferred_element_type=F32)


def _dot_tn(a, b):
    return lax.dot_general(a, b, (((0,), (0,)), ((), ())), preferred_element_type=F32)


def _resident(shape, index_map):
    return pl.BlockSpec(shape, index_map, pipeline_mode=pl.Buffered(1))


def _ffn_kernel(x_hbm, g_ref, wg_ref, wu_ref, wd_ref, *rest):
    if len(rest) == 7:
        gnext_ref, h_hbm, n_ref, acc_ref, xn_ref, in_sem, out_sem = rest
    else:
        (h_hbm, acc_ref, xn_ref, in_sem, out_sem), gnext_ref, n_ref = rest, None, None
    i, f = pl.program_id(0), pl.program_id(1)
    nt, nf = pl.num_programs(0), pl.num_programs(1)
    tm = xn_ref.shape[0]
    slot = i % 2

    def x_copy(tile, s):
        return pltpu.make_async_copy(x_hbm.at[pl.ds(tile * tm, tm)], acc_ref.at[s], in_sem.at[s])

    def h_copy(tile, s):
        return pltpu.make_async_copy(acc_ref.at[s], h_hbm.at[pl.ds(tile * tm, tm)], out_sem.at[s])

    @pl.when(f == 0)
    def _():
        @pl.when(i == 0)
        def _():
            x_copy(0, 0).start()

        x_copy(i, slot).wait()
        xn_ref[...] = _rms(acc_ref[slot], g_ref[...]).astype(BF16)

    @pl.when(f == 1)
    def _():
        @pl.when(i > 0)
        def _():
            h_copy(i - 1, 1 - slot).wait()

        @pl.when(i + 1 < nt)
        def _():
            x_copy(i + 1, 1 - slot).start()

    xn = xn_ref[...]
    gate = _dot(xn, wg_ref[...].astype(BF16))
    up = _dot(xn, wu_ref[...].astype(BF16))
    act = ((0.5 * gate) * jax.nn.sigmoid(gate) * up).astype(BF16)
    acc_ref[slot] += _dot(act, wd_ref[...].astype(BF16))

    @pl.when(f == nf - 1)
    def _():
        if n_ref is not None:
            n_ref[...] = _rms(acc_ref[slot], gnext_ref[...]).astype(BF16)
        h_copy(i, slot).start()

        @pl.when(i == nt - 1)
        def _():
            h_copy(i, slot).wait()


def _ffn(x, gain, wg, wu, wd, gain_next, *, tm, tf):
    t, d = x.shape
    dff = wg.shape[1]
    assert dff // tf >= 2, "the accumulator hand-over happens at the second F step"
    emit = gain_next is not None
    gain_spec = pl.BlockSpec((1, d), lambda i, f: (0, 0))
    return pl.pallas_call(
        _ffn_kernel,
        grid=(t // tm, dff // tf),
        in_specs=[
            pl.BlockSpec(memory_space=pl.ANY),
            gain_spec,
            pl.BlockSpec((d, tf), lambda i, f: (0, f)),
            pl.BlockSpec((d, tf), lambda i, f: (0, f)),
            pl.BlockSpec((tf, d), lambda i, f: (f, 0)),
        ] + [gain_spec] * emit,
        out_specs=[pl.BlockSpec(memory_space=pl.ANY)] + [pl.BlockSpec((tm, d), lambda i, f: (i, 0))] * emit,
        out_shape=[jax.ShapeDtypeStruct((t, d), F32)] + [jax.ShapeDtypeStruct((t, d), BF16)] * emit,
        scratch_shapes=[
            pltpu.VMEM((2, tm, d), F32),
            pltpu.VMEM((tm, d), BF16),
            pltpu.SemaphoreType.DMA((2,)),
            pltpu.SemaphoreType.DMA((2,)),
        ],
        compiler_params=pltpu.CompilerParams(
            dimension_semantics=("arbitrary", "arbitrary"),
            vmem_limit_bytes=FFN_VMEM_LIMIT_BYTES),
        name="ffn",
    )(x, gain, wg, wu, wd, *([gain_next] * emit))


def _bcast_row(x, blk, r):
    c, l = x.shape
    x3 = x.reshape(c // blk, blk, l)
    return jnp.broadcast_to(x3[:, r:r + 1, :], (c // blk, blk, l)).reshape(c, l)


def _chunk_cumsum(x, tri):
    c, w = tri.shape[0], x.shape[1]
    hi = x.astype(BF16)
    r1 = x - hi.astype(F32)
    mid = r1.astype(BF16)
    lo = (r1 - mid.astype(F32)).astype(BF16)
    parts = jnp.concatenate([hi, mid, lo], axis=1)
    cs = jnp.concatenate([_dot(tri, parts[c0:c0 + c]) for c0 in range(0, x.shape[0], c)], axis=0)
    return cs[:, 0:w] + cs[:, w:2 * w] + cs[:, 2 * w:3 * w]


def _split_distance(b2, l2, m, sub):
    c = b2.shape[0]
    if m == 2:
        return jnp.where(sub % 2 == 1, l2, 0.0)
    if m == 4:
        return b2 - jnp.where(sub < 4, _bcast_row(b2, SUBLANES, 1), _bcast_row(b2, SUBLANES, 5))
    if m == SUBLANES:
        return b2 - _bcast_row(b2, SUBLANES, 3)
    half = m // 2
    return jnp.concatenate(
        [b2[r0:r0 + m] - b2[r0 + half - 1:r0 + half] for r0 in range(0, c, m)], axis=0)


def _hgrn_tile(p_ref, r0, ts, lb, gn, state_t, o_ref, chunk, side_work, side_slots, every_split):
    starts = range(0, ts, chunk)
    rows = lambda x, c0: x[c0:c0 + chunk]
    q = p_ref[r0:r0 + ts, 0 * HEAD_DIM:1 * HEAD_DIM]
    fl = p_ref[r0:r0 + ts, 1 * HEAD_DIM:2 * HEAD_DIM]
    v = p_ref[r0:r0 + ts, 2 * HEAD_DIM:3 * HEAD_DIM].astype(BF16)
    g = p_ref[r0:r0 + ts, 3 * HEAD_DIM:4 * HEAD_DIM]

    tt = lax.broadcasted_iota(jnp.int32, (chunk, chunk), 0)
    ss = lax.broadcasted_iota(jnp.int32, (chunk, chunk), 1)
    tri = (ss <= tt).astype(BF16)
    diag = chunk.bit_length() - 1
    level = jnp.where(ss < tt, 31 - lax.clz(tt ^ ss), jnp.where(ss == tt, diag, -1))

    f = lb + (1.0 - lb) * jax.nn.sigmoid(fl)
    k = 1.0 - f
    qf = q * jax.nn.sigmoid(q)
    l2 = jnp.log2(f)
    def side(slot):
        for work, s in zip(side_work, side_slots):
            if s == slot:
                work()

    side(0)
    b2 = _chunk_cumsum(l2, tri)

    qb, kb = qf.astype(BF16), k.astype(BF16)
    sub = lax.broadcasted_iota(jnp.int32, (ts, HEAD_DIM), 0) % SUBLANES
    side(1)

    def split_scores(scores, first_m, first_li):
        m, li = first_m, first_li
        while m <= chunk:
            e = jnp.exp2(-jnp.abs(_split_distance(b2, l2, m, sub))).astype(BF16)
            qe, ke = qb * e, kb * e
            scores = [jnp.where(level == li, _dot_nt(rows(qe, c0), rows(ke, c0)), sc)
                      for c0, sc in zip(starts, scores)]
            m, li = 2 * m, li + 1
        return scores

    half = chunk // 2
    d_mid = jnp.concatenate(
        [b2[h0:h0 + half] - b2[h0 + half // 2 - 1:h0 + half // 2] for h0 in range(0, ts, half)], axis=0)
    near = jnp.max(jnp.abs(d_mid)) < MAX_PLAIN_LOG2_DECAY
    if every_split:
        scores = split_scores(
            [jnp.where(level == diag, _dot_nt(rows(qb, c0), rows(kb, c0)), 0.0) for c0 in starts], 2, 0)
    else:
        qe, ke = qb * jnp.exp2(d_mid).astype(BF16), kb * jnp.exp2(-d_mid).astype(BF16)
        inside = (level >= 0) & (level != diag - 1)
        scores = split_scores([jnp.where(inside, _dot_nt(rows(qe, c0), rows(ke, c0)), 0.0) for c0 in starts],
                              chunk, diag - 1)
    side(2)

    qs = qb * jnp.exp2(b2).astype(BF16)
    to_end = jnp.concatenate([b2[c0 + chunk - 1:c0 + chunk] - rows(b2, c0) for c0 in starts], axis=0)
    kd = kb * jnp.exp2(to_end).astype(BF16)
    o = []
    for c0, sc in zip(starts, scores):
        o.append(_dot(sc.astype(BF16), rows(v, c0)) + _dot_nt(rows(qs, c0), state_t.astype(BF16)))
        state_t = state_t * jnp.exp2(b2[c0 + chunk - 1:c0 + chunk]) + _dot_tn(rows(v, c0), rows(kd, c0))
    side(3)
    o = jnp.concatenate(o, axis=0)
    o_ref[0, r0:r0 + ts, :] = (_rms(o, gn) * (g * jax.nn.sigmoid(g))).astype(o_ref.dtype)
    return state_t, near


def _hgrn_kernel(u_ref, wq_ref, wf_ref, wi_ref, wg_ref, lbl_ref, gn_ref, o_ref,
                 w_ref, pa_ref, pb_ref, state_ref, sub_state_ref, *, chunk, sub_rows, piece_rows, piece_cols, piece_slots,
                 ni, n):
    j = pl.program_id(0)
    ts = pa_ref.shape[0]

    @pl.when(j == 0)
    def _():
        pb_ref[...] = jnp.zeros_like(pb_ref)
        state_ref[...] = jnp.zeros_like(state_ref)

    @pl.when((j % ni == 0) & (j < n))
    def _():
        for sec, src in enumerate((wq_ref, wf_ref, wi_ref, wg_ref)):
            w_ref[:, sec * HEAD_DIM:(sec + 1) * HEAD_DIM] = src[...].astype(BF16)

    def step(p_new_ref, p_ref):
        def piece(r0, c0):
            p_new_ref[r0:r0 + piece_rows, c0:c0 + piece_cols] = _dot(
                u_ref[0, r0:r0 + piece_rows, :], w_ref[:, c0:c0 + piece_cols])

        pieces = [functools.partial(piece, r0, c0)
                  for r0 in range(0, ts, piece_rows) for c0 in range(0, 4 * HEAD_DIM, piece_cols)]

        logits = [lbl_ref[l, 0] for l in range(lbl_ref.shape[0])]
        mx = functools.reduce(jnp.maximum, logits)
        ex = [jnp.exp(l - mx) for l in logits]
        lb = ex[0] / functools.reduce(jnp.add, ex)

        first = (jnp.maximum(j - 1, 0) % ni) == 0
        state_t = jnp.where(first, 0.0, state_ref[...])
        n_sub = ts // sub_rows
        per = len(pieces) // n_sub
        all_near = True
        for si in range(n_sub):
            sub_state_ref[si] = state_t
            state_t, near = _hgrn_tile(p_ref, si * sub_rows, sub_rows, lb, gn_ref[...], state_t, o_ref, chunk,
                                       pieces[si * per:(si + 1) * per], piece_slots, every_split=False)
            all_near = jnp.logical_and(all_near, near)
        state_ref[...] = state_t

        @pl.when(jnp.logical_not(all_near))
        def _():
            for si in range(n_sub):
                _hgrn_tile(p_ref, si * sub_rows, sub_rows, lb, gn_ref[...], sub_state_ref[si], o_ref, chunk,
                           (), (), every_split=True)

    @pl.when(j % 2 == 0)
    def _():
        step(pa_ref, pb_ref)

    @pl.when(j % 2 == 1)
    def _():
        step(pb_ref, pa_ref)


def _hgrn(u, w_in, lb_logits, gn, *, heads, ts, chunk, sub_rows, piece_rows, piece_cols, piece_slots):
    bsz, s, d = u.shape
    nl = lb_logits.shape[0]
    ni = s // ts
    n = bsz * heads * ni

    def cur(j):
        jc = jnp.minimum(j, n - 1)
        return jc // (heads * ni), (jc // ni) % heads, jc % ni

    def prev(j):
        return cur(jnp.maximum(j - 1, 0))

    w_spec = lambda sec: pl.BlockSpec((d, HEAD_DIM), lambda j: (0, sec * heads + cur(j)[1]))
    return pl.pallas_call(
        functools.partial(_hgrn_kernel, chunk=chunk, sub_rows=sub_rows, piece_rows=piece_rows, piece_cols=piece_cols,
                          piece_slots=piece_slots, ni=ni, n=n),
        grid=(n + 1,),
        in_specs=[
            pl.BlockSpec((1, ts, d), lambda j: (cur(j)[0], cur(j)[2], 0)),
            w_spec(0), w_spec(1), w_spec(2), w_spec(3),
            pl.BlockSpec((nl, 1, 1, HEAD_DIM), lambda j: (0, prev(j)[1], 0, 0)),
            pl.BlockSpec((1, HEAD_DIM), lambda j: (0, 0)),
        ],
        out_specs=pl.BlockSpec((1, ts, HEAD_DIM), lambda j: (prev(j)[0], prev(j)[2], prev(j)[1])),
        out_shape=jax.ShapeDtypeStruct((bsz, s, heads * HEAD_DIM), BF16),
        scratch_shapes=[
            pltpu.VMEM((d, 4 * HEAD_DIM), BF16),
            pltpu.VMEM((ts, 4 * HEAD_DIM), F32),
            pltpu.VMEM((ts, 4 * HEAD_DIM), F32),
            pltpu.VMEM((HEAD_DIM, HEAD_DIM), F32),
            pltpu.VMEM((ts // sub_rows, HEAD_DIM, HEAD_DIM), F32),
        ],
        compiler_params=pltpu.CompilerParams(
            dimension_semantics=("arbitrary",),
            vmem_limit_bytes=VMEM_LIMIT_BYTES),
        name="hgrn",
    )(u, w_in, w_in, w_in, w_in, lb_logits, gn)


def _conv_kernel(u_ref, wb_ref, wc_ref, wv_ref, cw_ref, gn_ref, o_ref,
                 w_ref, pa_ref, pb_ref, halo_ref, *, ni, sweep, n):
    j = pl.program_id(0)
    ts, gw = o_ref.shape[1], o_ref.shape[2]

    @pl.when(j == 0)
    def _():
        pb_ref[...] = jnp.zeros_like(pb_ref)
        halo_ref[...] = jnp.zeros_like(halo_ref)

    @pl.when((j % sweep == 0) & (j < n))
    def _():
        for sec, src in enumerate((wb_ref, wc_ref, wv_ref)):
            w_ref[:, sec * gw:(sec + 1) * gw] = src[...].astype(BF16)

    def step(p_new_ref, p_ref):
        p_new_ref[...] = _dot(u_ref[0], w_ref[...])

        first = (jnp.maximum(j - 1, 0) % ni) == 0
        bg = p_ref[:, 0:gw]
        uc = p_ref[:, gw:2 * gw] * p_ref[:, 2 * gw:3 * gw]
        row = lax.broadcasted_iota(jnp.int32, (ts, gw), 0)
        prev1 = jnp.where(first, 0.0, halo_ref[1:2, :])
        prev2 = jnp.where(first, 0.0, halo_ref[0:1, :])
        u1 = jnp.where(row >= 1, pltpu.roll(uc, 1, axis=0), prev1)
        u2 = jnp.where(row >= 2, pltpu.roll(uc, 2, axis=0), jnp.where(row == 1, prev1, prev2))
        halo_ref[...] = uc[ts - 2:ts, :]
        y = bg * (cw_ref[0:1, :] * u2 + cw_ref[1:2, :] * u1 + cw_ref[2:3, :] * uc)
        for g0 in range(0, gw, HEAD_DIM):
            yg = y[:, g0:g0 + HEAD_DIM]
            o_ref[0, :, g0:g0 + HEAD_DIM] = _rms(yg, gn_ref[:, g0:g0 + HEAD_DIM]).astype(o_ref.dtype)

    @pl.when(j % 2 == 0)
    def _():
        step(pa_ref, pb_ref)

    @pl.when(j % 2 == 1)
    def _():
        step(pb_ref, pa_ref)


def _conv(u, w_in, conv_w, gn, *, col0, ts, gw):
    bsz, s, d = u.shape
    cw = conv_w.shape[1]
    ni = s // ts
    sweep = bsz * ni
    n = (cw // gw) * sweep

    def cur(j):
        jc = jnp.minimum(j, n - 1)
        return jc // sweep, (jc // ni) % bsz, jc % ni

    def prev(j):
        return cur(jnp.maximum(j - 1, 0))

    w_spec = lambda sec: pl.BlockSpec((d, gw), lambda j: (0, (col0 + sec * cw) // gw + cur(j)[0]))
    return pl.pallas_call(
        functools.partial(_conv_kernel, ni=ni, sweep=sweep, n=n),
        grid=(n + 1,),
        in_specs=[
            pl.BlockSpec((1, ts, d), lambda j: (cur(j)[1], cur(j)[2], 0)),
            w_spec(0), w_spec(1), w_spec(2),
            pl.BlockSpec((3, gw), lambda j: (0, prev(j)[0])),
            pl.BlockSpec((1, gw), lambda j: (0, prev(j)[0])),
        ],
        out_specs=pl.BlockSpec((1, ts, gw), lambda j: (prev(j)[1], prev(j)[2], prev(j)[0])),
        out_shape=jax.ShapeDtypeStruct((bsz, s, cw), BF16),
        scratch_shapes=[
            pltpu.VMEM((d, 3 * gw), BF16),
            pltpu.VMEM((ts, 3 * gw), F32),
            pltpu.VMEM((ts, 3 * gw), F32),
            pltpu.VMEM((2, gw), F32),
        ],
        compiler_params=pltpu.CompilerParams(
            dimension_semantics=("arbitrary",),
            vmem_limit_bytes=VMEM_LIMIT_BYTES),
        name="conv",
    )(u, w_in, w_in, w_in, conv_w, gn)


def _outproj_kernel(h_ref, a_ref, b_ref, wo_ref, o_ref, w_ref):
    @pl.when(pl.program_id(0) == 0)
    def _():
        w_ref[...] = wo_ref[...].astype(BF16)

    ka = a_ref.shape[1]
    o_ref[...] = h_ref[...] + _dot(a_ref[...], w_ref[0:ka, :]) + _dot(b_ref[...], w_ref[ka:, :])


def _outproj(h, a, b, wo, *, tm):
    t, d = h.shape
    ka, kb = a.shape[1], b.shape[1]
    return pl.pallas_call(
        _outproj_kernel,
        grid=(t // tm,),
        in_specs=[
            pl.BlockSpec((tm, d), lambda i: (i, 0)),
            pl.BlockSpec((tm, ka), lambda i: (i, 0)),
            pl.BlockSpec((tm, kb), lambda i: (i, 0)),
            _resident((ka + kb, d), lambda i: (0, 0)),
        ],
        out_specs=pl.BlockSpec((tm, d), lambda i: (i, 0)),
        out_shape=jax.ShapeDtypeStruct((t, d), F32),
        scratch_shapes=[pltpu.VMEM((ka + kb, d), BF16)],
        compiler_params=pltpu.CompilerParams(
            dimension_semantics=("arbitrary",),
            vmem_limit_bytes=VMEM_LIMIT_BYTES),
        name="outproj",
    )(h, a, b, wo)


def _ple_kernel(h_ref, gn_ref, p_ref, wg_ref, we_ref, gf_ref, o_ref, w_ref):
    @pl.when(pl.program_id(0) == 0)
    def _():
        w_ref[...] = wg_ref[...].astype(BF16)

    h = h_ref[...]
    gate = jax.nn.sigmoid(_dot(_rms(h, gn_ref[...]).astype(BF16), w_ref[...]))
    pe = _dot(p_ref[...].astype(BF16), we_ref[...].astype(BF16))
    o_ref[...] = _rms(h + gate * pe, gf_ref[...])


def _ple(h, gn, p, wg, we, gf, *, tm):
    t, d = h.shape
    pd = p.shape[1]
    return pl.pallas_call(
        _ple_kernel,
        grid=(t // tm,),
        in_specs=[
            pl.BlockSpec((tm, d), lambda i: (i, 0)),
            pl.BlockSpec((1, d), lambda i: (0, 0)),
            pl.BlockSpec((tm, pd), lambda i: (i, 0)),
            _resident((d, d), lambda i: (0, 0)),
            _resident((pd, d), lambda i: (0, 0)),
            pl.BlockSpec((1, d), lambda i: (0, 0)),
        ],
        out_specs=pl.BlockSpec((tm, d), lambda i: (i, 0)),
        out_shape=jax.ShapeDtypeStruct((t, d), F32),
        scratch_shapes=[pltpu.VMEM((d, d), BF16)],
        compiler_params=pltpu.CompilerParams(
            dimension_semantics=("arbitrary",),
            vmem_limit_bytes=VMEM_LIMIT_BYTES),
        name="ple",
    )(h, gn, p, wg, we, gf)


def kernel(x, p, norm_ffn1, ffn1_gate, ffn1_up, ffn1_down, norm_mix, w_in, conv_w, hgrn_lb_logits,
           hgrn_norm, conv_norm, w_out, norm_ffn2, ffn2_gate, ffn2_up, ffn2_down, norm_ple, w_ple,
           w_ple_gate, norm_final):
    bsz, s, d = x.shape
    depth = w_in.shape[0]
    assert depth == 1, "single-layer block"
    t = bsz * s
    hw = hgrn_lb_logits.shape[1]
    heads = hw // HEAD_DIM
    cw = conv_w.shape[2]
    assert w_in.shape[2] == 4 * hw + 3 * cw

    row = lambda a: a.reshape(1, -1)
    i = 0
    lb_logits = hgrn_lb_logits.reshape(-1, heads, 1, HEAD_DIM)

    tl = TILES
    h1, u = _ffn(x.reshape(t, d), row(norm_ffn1[i]), ffn1_gate[i], ffn1_up[i], ffn1_down[i],
                 row(norm_mix[i]), tm=tl.ffn_tokens, tf=tl.ffn_hidden)
    u3 = u.reshape(bsz, s, d)
    o_h = _hgrn(u3, w_in[i], lb_logits, row(hgrn_norm[i]), heads=heads, ts=tl.hgrn_seq, chunk=tl.hgrn_chunk,
                sub_rows=tl.hgrn_sub, piece_rows=tl.hgrn_piece[0], piece_cols=tl.hgrn_piece[1],
                piece_slots=tl.hgrn_piece_slots)
    y_c = _conv(u3, w_in[i], conv_w[i], row(conv_norm[i]), col0=4 * hw, ts=tl.conv_seq, gw=tl.conv_channels)
    h2 = _outproj(h1, o_h.reshape(t, hw), y_c.reshape(t, cw), w_out[i], tm=tl.proj_tokens)
    (h3,) = _ffn(h2, row(norm_ffn2[i]), ffn2_gate[i], ffn2_up[i], ffn2_down[i], None,
                 tm=tl.ffn_tokens, tf=tl.ffn_hidden)
    y = _ple(h3, row(norm_ple[i]), p[i].reshape(t, -1), w_ple_gate[i], w_ple[i], row(norm_final),
             tm=tl.proj_tokens)
    return y.reshape(bsz, s, d)
```
